```python
import jax, jax.numpy as jnp
from jax import lax
import numpy as np

D_MODEL = 1024
BATCH = 4
SEQ = 8192
DEPTH = 2

HEAD_DIM = 64
POOL_WIDTH = D_MODEL // 4
POOL_WINDOWS = (2, 4, 8, 16)
N_POOL_GROUPS = len(POOL_WINDOWS)
POOL_GC = POOL_WIDTH // N_POOL_GROUPS
ATTN_WIDTH = D_MODEL - POOL_WIDTH
N_ATTN_HEADS = ATTN_WIDTH // HEAD_DIM
DIL_PATTERNS = ((128, 1), (512, 4), (2048, 16))
HEADS_PER_PATTERN = N_ATTN_HEADS // len(DIL_PATTERNS)
ROT_DIM = HEAD_DIM // 4
ROPE_THETA = 500000.0
BLK = 128
D_FF = 4 * D_MODEL
PLE_DIM = 256
EPS = 1e-6

kernel_name = "hybrid_pool_dilated_attn_block"


def rmsnorm(x, g):
    xf = x.astype(jnp.float32)
    y = xf * lax.rsqrt(jnp.mean(xf * xf, axis=-1, keepdims=True) + EPS)
    return (y * g.astype(jnp.float32)).astype(x.dtype)


def rotary_tables(positions, dtype):
    inv_freq = ROPE_THETA ** (-jnp.arange(0, ROT_DIM, 2, dtype=jnp.float32) / ROT_DIM)
    ang = positions.astype(jnp.float32)[..., None] * inv_freq
    return jnp.cos(ang)[:, :, None, :].astype(dtype), jnp.sin(ang)[:, :, None, :].astype(dtype)


def apply_partial_rotary(x, cos, sin):
    half = ROT_DIM // 2
    x1 = x[..., :half]
    x2 = x[..., half:ROT_DIM]
    rot = jnp.concatenate([x1 * cos - x2 * sin, x2 * cos + x1 * sin], axis=-1)
    return jnp.concatenate([rot, x[..., ROT_DIM:]], axis=-1)


def pool_mixer(u, w, scale):
    B, S, _ = u.shape
    ug = u.reshape(B, S, N_POOL_GROUPS, POOL_GC).astype(jnp.float32)
    c = lax.cumsum(ug, axis=1)
    c0 = jnp.pad(c, ((0, 0), (1, 0), (0, 0), (0, 0)))
    t = jnp.arange(S, dtype=jnp.int32)
    win = jnp.array(POOL_WINDOWS, dtype=jnp.int32)
    lo = jnp.maximum(t[:, None] + 1 - win[None, :], 0)
    c_lo = jnp.take_along_axis(c0, lo[None, :, :, None], axis=1)
    cnt = (t[:, None] + 1 - lo).astype(jnp.float32)
    y = ((c - c_lo) / cnt[None, :, :, None] - ug).astype(u.dtype)
    y = jnp.einsum('bsgc,gcd->bsgd', y, w).reshape(B, S, POOL_WIDTH)
    return y * scale


def dilated_window_attention(q, k, v, window, dil):
    B, S, H, Dh = q.shape
    steps = window // dil
    L = -(-S // (dil * BLK)) * BLK
    pad = L * dil - S
    nb = L // BLK

    def to_strided(a):
        a = jnp.pad(a, ((0, 0), (0, pad), (0, 0), (0, 0)))
        a = a.reshape(B, L, dil, H, Dh).transpose(0, 2, 1, 3, 4)
        return a.reshape(B, dil, nb, BLK, H, Dh)

    def with_prev(a):
        prev = jnp.pad(a[:, :, :-1], ((0, 0), (0, 0), (1, 0), (0, 0), (0, 0), (0, 0)))
        return jnp.concatenate([prev, a], axis=3)

    qs = to_strided(q)
    kb = with_prev(to_strided(k))
    vb = with_prev(to_strided(v))
    s = jnp.einsum('brnqhd,brnkhd->brnhqk', qs, kb,
                   preferred_element_type=jnp.float32) * (HEAD_DIM ** -0.5)
    qi = jnp.arange(nb)[:, None] * BLK + jnp.arange(BLK)[None, :]
    ki = jnp.arange(nb)[:, None] * BLK - BLK + jnp.arange(2 * BLK)[None, :]
    dist = qi[:, :, None] - ki[:, None, :]
    valid = (dist >= 0) & (dist <= steps) & (ki[:, None, :] >= 0)
    s = jnp.where(valid[None, None, :, None], s, -jnp.inf)
    m = jnp.max(s, axis=-1, keepdims=True)
    e = jnp.exp(s - m)
    l = jnp.sum(e, axis=-1, keepdims=True)
    o = jnp.einsum('brnhqk,brnkhd->brnqhd', e / l, vb.astype(jnp.float32))
    lse = (m + jnp.log(l))[..., 0]
    o = o.reshape(B, dil, L, H, Dh).transpose(0, 2, 1, 3, 4).reshape(B, L * dil, H, Dh)[:, :S]
    lse = lse.transpose(0, 1, 2, 4, 3).reshape(B, dil, L, H).transpose(0, 2, 1, 3)
    lse = lse.reshape(B, L * dil, H)[:, :S]
    return o, lse


def dilated_mixer(q, k, v):
    outs, lses = [], []
    for g, (window, dil) in enumerate(DIL_PATTERNS):
        sl = slice(g * HEADS_PER_PATTERN, (g + 1) * HEADS_PER_PATTERN)
        o, lse = dilated_window_attention(q[:, :, sl], k[:, :, sl], v[:, :, sl], window, dil)
        outs.append(o)
        lses.append(lse)
    wts = jax.nn.softmax(jnp.stack(lses, axis=0), axis=0)
    o = jnp.concatenate([outs[g] * wts[g][..., None] for g in range(len(DIL_PATTERNS))], axis=2)
    B, S = q.shape[0], q.shape[1]
    return o.reshape(B, S, ATTN_WIDTH).astype(q.dtype)


def setup_inputs(seed: int = 0) -> dict:
    key = jax.random.key(seed)
    ks = jax.random.split(key, 16)
    f32 = jnp.float32
    n_in = POOL_WIDTH + 3 * ATTN_WIDTH
    return {
        "x": jax.random.normal(ks[0], (BATCH, SEQ, D_MODEL), f32),
        "p": jax.random.normal(ks[1], (DEPTH, BATCH, SEQ, PLE_DIM), f32),
        "positions": jnp.broadcast_to(jnp.arange(SEQ, dtype=jnp.int32), (BATCH, SEQ)),
        "norm1": 1.0 + 0.02 * jax.random.normal(ks[2], (DEPTH, D_MODEL), f32),
        "w_in": jax.random.normal(ks[3], (DEPTH, D_MODEL, n_in), f32) * D_MODEL ** -0.5,
        "pool_w": jax.random.normal(ks[4], (DEPTH, N_POOL_GROUPS, POOL_GC, POOL_GC), f32) * POOL_GC ** -0.5,
        "pool_scale": 1.0 + 0.02 * jax.random.normal(ks[5], (DEPTH, POOL_WIDTH), f32),
        "w_out": jax.random.normal(ks[6], (DEPTH, POOL_WIDTH + ATTN_WIDTH, D_MODEL), f32) * (POOL_WIDTH + ATTN_WIDTH) ** -0.5,
        "norm2": 1.0 + 0.02 * jax.random.normal(ks[7], (DEPTH, D_MODEL), f32),
        "w_up": jax.random.normal(ks[8], (DEPTH, D_MODEL, D_FF), f32) * D_MODEL ** -0.5,
        "w_down": jax.random.normal(ks[9], (DEPTH, D_FF, D_MODEL), f32) * D_FF ** -0.5,
        "norm3": 1.0 + 0.02 * jax.random.normal(ks[10], (DEPTH, D_MODEL), f32),
        "w_gate": jax.random.normal(ks[11], (DEPTH, D_MODEL, D_MODEL), f32) * D_MODEL ** -0.5,
        "w_ple": jax.random.normal(ks[12], (DEPTH, PLE_DIM, D_MODEL), f32) * PLE_DIM ** -0.5,
        "final_norm": 1.0 + 0.02 * jax.random.normal(ks[13], (D_MODEL,), f32),
    }


def reference(x, p, positions, norm1, w_in, pool_w, pool_scale, w_out, norm2, w_up, w_down,
              norm3, w_gate, w_ple, final_norm):
    B, S, _ = x.shape
    cos, sin = rotary_tables(positions, x.dtype)
    h = x
    for i in range(DEPTH):
        hn = rmsnorm(h, norm1[i])
        z = hn @ w_in[i]
        u = z[..., :POOL_WIDTH]
        q, k, v = jnp.split(z[..., POOL_WIDTH:], 3, axis=-1)
        q = apply_partial_rotary(q.reshape(B, S, N_ATTN_HEADS, HEAD_DIM), cos, sin)
        k = apply_partial_rotary(k.reshape(B, S, N_ATTN_HEADS, HEAD_DIM), cos, sin)
        v = v.reshape(B, S, N_ATTN_HEADS, HEAD_DIM)
        pool_out = pool_mixer(u, pool_w[i], pool_scale[i])
        attn_out = dilated_mixer(q, k, v)
        h = h + jnp.concatenate([pool_out, attn_out], axis=-1) @ w_out[i]
        hn = rmsnorm(h, norm2[i])
        h = h + jnp.square(jax.nn.relu(hn @ w_up[i])) @ w_down[i]
        gate = jax.nn.sigmoid(rmsnorm(h, norm3[i]) @ w_gate[i])
        h = h + gate * (p[i] @ w_ple[i])
    return rmsnorm(h, final_norm)
```

```python
import functools

import jax
import jax.numpy as jnp
from jax import lax
from jax.experimental import pallas as pl
from jax.experimental.pallas import tpu as pltpu

F32 = jnp.float32
BF16 = jnp.bfloat16

D_MODEL = 1024
HEAD_DIM = 64
POOL_WIDTH = 256
POOL_WINDOWS = (2, 4, 8, 16)
POOL_GC = POOL_WIDTH // len(POOL_WINDOWS)
ATTN_WIDTH = D_MODEL - POOL_WIDTH
DILATIONS = (1, 4, 16)
GROUP_WIDTH = ATTN_WIDTH // len(DILATIONS)
ROT_DIM = HEAD_DIM // 4
ROPE_THETA = 500000.0
BLK = 128
D_FF = 4 * D_MODEL
EPS = 1e-6
MASK_VALUE = -1e30

LANES = 128
POOL_HALO = 16
ROW_TILE = 512
ATTN_ROWS = 2048
ATTN_MIN_ROWS = 512
FF_CHUNK = 1024
VMEM_LIMIT = 56 * 1024 * 1024


def _rms(x, g):
    return x * lax.rsqrt(jnp.mean(x * x, axis=-1, keepdims=True) + EPS) * g


def _const_spec(shape):
    zeros = (0,) * len(shape)
    return pl.BlockSpec(shape, lambda *_: zeros, pipeline_mode=pl.Buffered(1))


def _rotary(x, cos_t, sin_t, first_half):
    half = ROT_DIM // 2
    partner = jnp.where(first_half, pltpu.roll(x, LANES - half, 1), pltpu.roll(x, half, 1))
    return x * cos_t + partner * sin_t


def _inproj_kernel(x_ref, g_ref, w_ref, cos_ref, sin_ref, u_ref, *qkv_refs):
    hn = _rms(x_ref[0], g_ref[...]).astype(BF16)
    cos_t = cos_ref[0]
    sin_t = sin_ref[0]
    lane = lax.broadcasted_iota(jnp.int32, cos_t.shape, 1)
    first_half = (lane % HEAD_DIM) < (ROT_DIM // 2)

    u_ref[0] = jnp.dot(hn, w_ref[:, :POOL_WIDTH], preferred_element_type=F32)
    for j, out_ref in enumerate(qkv_refs):
        col = POOL_WIDTH + j * GROUP_WIDTH
        z = jnp.dot(hn, w_ref[:, col:col + GROUP_WIDTH], preferred_element_type=F32)
        if j < 2 * len(DILATIONS):
            scale = HEAD_DIM ** -0.5 if j < len(DILATIONS) else 1.0
            halves = [_rotary(z[:, c:c + LANES], cos_t, sin_t, first_half) * scale
                      for c in range(0, GROUP_WIDTH, LANES)]
            z = jnp.concatenate(halves, axis=-1)
        out_ref[0] = z.astype(out_ref.dtype)


def _inproj(x, g, w_in, cos_t, sin_t):
    B, S, D = x.shape
    n_in = w_in.shape[1]
    tm = ROW_TILE
    row = lambda width: pl.BlockSpec((1, tm, width), lambda b, i: (b, i, 0))
    out_shape = [jax.ShapeDtypeStruct((B, S, POOL_WIDTH), F32)]
    out_shape += [jax.ShapeDtypeStruct((B, S, GROUP_WIDTH), BF16)] * (3 * len(DILATIONS))
    return pl.pallas_call(
        _inproj_kernel,
        grid=(B, S // tm),
        in_specs=[row(D), _const_spec((1, D)), _const_spec((D, n_in)), row(LANES), row(LANES)],
        out_specs=[row(POOL_WIDTH)] + [row(GROUP_WIDTH)] * (3 * len(DILATIONS)),
        out_shape=out_shape,
        compiler_params=pltpu.CompilerParams(
            dimension_semantics=("parallel", "parallel"), vmem_limit_bytes=VMEM_LIMIT),
        name="inproj",
    )(x, g, w_in, cos_t, sin_t)


def _attn_kernel(q_ref, k_ref, kh_ref, v_ref, vh_ref, bias_ref, o_ref, lse_ref, kbuf, vbuf):
    rows, cols = q_ref.shape[1], q_ref.shape[2]
    n_blk = rows // BLK
    first_chunk = pl.program_id(1) == 0

    kbuf[0:BLK] = kh_ref[0]
    kbuf[BLK:] = k_ref[0]
    vbuf[0:BLK] = vh_ref[0]
    vbuf[BLK:] = v_ref[0]

    lane = lax.broadcasted_iota(jnp.int32, (BLK, LANES), 1)
    head0 = lane < HEAD_DIM

    for c in range(0, cols, LANES):
        def block(i, carry, c=c):
            r0 = pl.multiple_of(i * BLK, BLK)
            q = q_ref[0, pl.ds(r0, BLK), c:c + LANES]
            kwin = kbuf[pl.ds(r0, 2 * BLK), c:c + LANES]
            vwin = vbuf[pl.ds(r0, 2 * BLK), c:c + LANES]
            zero = jnp.zeros_like(q)
            q2 = jnp.concatenate([jnp.where(head0, q, zero), jnp.where(head0, zero, q)], axis=0)
            s = lax.dot_general(q2, kwin, (((1,), (1,)), ((), ())), preferred_element_type=F32)
            no_history = jnp.logical_and(first_chunk, i == 0).astype(jnp.int32)
            s = s + bias_ref[no_history]
            m = jnp.max(s, axis=-1, keepdims=True)
            e = jnp.exp(s - m)
            l = jnp.sum(e, axis=-1, keepdims=True)
            pv = jnp.dot(e.astype(BF16), vwin, preferred_element_type=F32)
            pv = pv / l
            lse = m + jnp.log(l)
            o_ref[0, pl.ds(r0, BLK), c:c + LANES] = jnp.where(head0, pv[:BLK], pv[BLK:]).astype(o_ref.dtype)
            lse_ref[0, pl.ds(r0, BLK), c:c + LANES] = jnp.where(
                head0, jnp.broadcast_to(lse[:BLK], (BLK, LANES)), jnp.broadcast_to(lse[BLK:], (BLK, LANES)))
            return carry
        lax.fori_loop(0, n_blk, block, 0)


def _attention(q, k, v, bias, dil):
    B, S, W = q.shape
    L = S // dil
    rows = min(L, max(ATTN_ROWS // dil, ATTN_MIN_ROWS))
    cols = (ATTN_ROWS // rows) * W
    view = lambda a: a.reshape(B, L, dil * W)
    hist_blocks = rows // BLK
    main = pl.BlockSpec((1, rows, cols), lambda b, n, r: (b, n, r))
    hist = pl.BlockSpec((1, BLK, cols), lambda b, n, r: (b, jnp.maximum(n * hist_blocks - 1, 0), r))
    o, lse = pl.pallas_call(
        _attn_kernel,
        grid=(B, L // rows, dil * W // cols),
        in_specs=[main, main, hist, main, hist, _const_spec(bias.shape)],
        out_specs=[main, main],
        out_shape=[jax.ShapeDtypeStruct((B, L, dil * W), BF16), jax.ShapeDtypeStruct((B, L, dil * W), F32)],
        scratch_shapes=[pltpu.VMEM((BLK + rows, cols), BF16), pltpu.VMEM((BLK + rows, cols), BF16)],
        compiler_params=pltpu.CompilerParams(
            dimension_semantics=("parallel", "arbitrary", "parallel"), vmem_limit_bytes=VMEM_LIMIT),
        name=f"attn_d{dil}",
    )(view(q), view(k), view(k), view(v), view(v), bias)
    return o.reshape(B, S, W), lse.reshape(B, S, W)


def _band_bias():
    a = jnp.arange(2 * BLK)[:, None] % BLK
    c = jnp.arange(2 * BLK)[None, :]
    band = (c >= a) & (c <= a + BLK)
    both = jnp.stack([band, band & (c >= BLK)])
    return jnp.where(both, 0.0, MASK_VALUE).astype(F32)


def _mix_kernel(h_ref, u_ref, uh_ref, o1_ref, o4_ref, o16_ref, l1_ref, l4_ref, l16_ref, p_ref,
                wpool_ref, pscale_ref, wout_ref, g2_ref, wup_ref, wdown_ref, g3_ref, wgate_ref, wple_ref,
                gf_ref, out_ref, *, final):
    tm = h_ref.shape[1]
    first_tile = pl.program_id(1) == 0

    hist = jnp.where(first_tile, 0.0, uh_ref[0])
    ub = jnp.concatenate([hist, u_ref[0]], axis=0)
    s2 = ub + pltpu.roll(ub, 1, 0)
    s4 = s2 + pltpu.roll(s2, 2, 0)
    s8 = s4 + pltpu.roll(s4, 4, 0)
    s16 = s8 + pltpu.roll(s8, 8, 0)
    t = lax.broadcasted_iota(jnp.int32, (tm, LANES), 0) + pl.program_id(1) * tm + 1
    lane = lax.broadcasted_iota(jnp.int32, (tm, LANES), 1)
    low = lane < POOL_GC
    cnt = lambda w: jnp.minimum(t, w).astype(F32)
    tile = lambda s, c: s[POOL_HALO:, c:c + LANES]
    y_lo = jnp.where(low, tile(s2, 0) / cnt(2), tile(s4, 0) / cnt(4)) - tile(ub, 0)
    y_hi = jnp.where(low, tile(s8, LANES) / cnt(8), tile(s16, LANES) / cnt(16)) - tile(ub, LANES)
    y = jnp.concatenate([y_lo, y_hi], axis=-1).astype(BF16)
    pool = jnp.dot(y, wpool_ref[...], preferred_element_type=F32) * pscale_ref[...]

    l1, l4, l16 = l1_ref[0], l4_ref[0], l16_ref[0]
    mx = jnp.maximum(jnp.maximum(l1, l4), l16)
    e1, e4, e16 = jnp.exp(l1 - mx), jnp.exp(l4 - mx), jnp.exp(l16 - mx)
    inv = 1.0 / (e1 + e4 + e16)
    mix = jnp.concatenate([
        pool.astype(BF16),
        (o1_ref[0].astype(F32) * (e1 * inv)).astype(BF16),
        (o4_ref[0].astype(F32) * (e4 * inv)).astype(BF16),
        (o16_ref[0].astype(F32) * (e16 * inv)).astype(BF16)], axis=-1)
    h = h_ref[0] + jnp.dot(mix, wout_ref[...], preferred_element_type=F32)

    hn = _rms(h, g2_ref[...]).astype(BF16)
    acc = h
    for c in range(0, D_FF, FF_CHUNK):
        up = jnp.dot(hn, wup_ref[:, c:c + FF_CHUNK], preferred_element_type=F32)
        act = jnp.square(jnp.maximum(up, 0.0)).astype(BF16)
        acc = acc + jnp.dot(act, wdown_ref[c:c + FF_CHUNK, :], preferred_element_type=F32)
    h = acc

    hn = _rms(h, g3_ref[...]).astype(BF16)
    gate = 1.0 / (1.0 + jnp.exp(-jnp.dot(hn, wgate_ref[...], preferred_element_type=F32)))
    ple = jnp.dot(p_ref[0].astype(BF16), wple_ref[...], preferred_element_type=F32)
    h = h + gate * ple
    if final:
        h = _rms(h, gf_ref[...])
    out_ref[0] = h


def _mix(h, u, o, lse, p, wpool, pscale, wout, g2, wup, wdown, g3, wgate, wple, gf, final):
    B, S, D = h.shape
    tm = ROW_TILE
    row = lambda width: pl.BlockSpec((1, tm, width), lambda b, i: (b, i, 0))
    halo = pl.BlockSpec((1, POOL_HALO, POOL_WIDTH),
                        lambda b, i: (b, jnp.maximum(i * (tm // POOL_HALO) - 1, 0), 0))
    weights = [wpool, pscale, wout, g2, wup, wdown, g3, wgate, wple, gf]
    return pl.pallas_call(
        functools.partial(_mix_kernel, final=final),
        grid=(B, S // tm),
        in_specs=[row(D), row(POOL_WIDTH), halo] + [row(GROUP_WIDTH)] * 6 + [row(p.shape[-1])]
                 + [_const_spec(w.shape) for w in weights],
        out_specs=row(D),
        out_shape=jax.ShapeDtypeStruct((B, S, D), F32),
        compiler_params=pltpu.CompilerParams(
            dimension_semantics=("parallel", "arbitrary"), vmem_limit_bytes=VMEM_LIMIT),
        name="mix_final" if final else "mix",
    )(h, u, u, *o, *lse, p, *weights)


def _rotary_tables(positions):
    half = ROT_DIM // 2
    inv_freq = ROPE_THETA ** (-jnp.arange(0, ROT_DIM, 2, dtype=F32) / ROT_DIM)
    ang = positions.astype(F32)[..., None] * inv_freq
    cos, sin = jnp.cos(ang), jnp.sin(ang)
    rest = positions.shape + (HEAD_DIM - ROT_DIM,)
    cos_h = jnp.concatenate([cos, cos, jnp.ones(rest, F32)], axis=-1)
    sin_h = jnp.concatenate([-sin, sin, jnp.zeros(rest, F32)], axis=-1)
    reps = LANES // HEAD_DIM
    return jnp.tile(cos_h, (1, 1, reps)), jnp.tile(sin_h, (1, 1, reps))


def _block_diag(w):
    g, c, _ = w.shape
    eye = jnp.eye(g, dtype=w.dtype)
    return (w[:, :, None, :] * eye[:, None, :, None]).reshape(g * c, g * c)


def kernel(x, p, positions, norm1, w_in, pool_w, pool_scale, w_out, norm2, w_up, w_down, norm3, w_gate,
           w_ple, final_norm):
    depth = w_in.shape[0]
    n_grp = len(DILATIONS)
    cos_t, sin_t = _rotary_tables(positions)
    bias = _band_bias()
    vec = lambda a: a.reshape(1, -1)
    h = x
    for i in range(depth):
        u, *qkv = _inproj(h, vec(norm1[i]), w_in[i].astype(BF16), cos_t, sin_t)
        o, lse = [], []
        for g, dil in enumerate(DILATIONS):
            og, lg = _attention(qkv[g], qkv[n_grp + g], qkv[2 * n_grp + g], bias, dil)
            o.append(og)
            lse.append(lg)
        h = _mix(h, u, o, lse, p[i], _block_diag(pool_w[i]).astype(BF16), vec(pool_scale[i]),
                 w_out[i].astype(BF16), vec(norm2[i]), w_up[i].astype(BF16), w_down[i].astype(BF16),
                 vec(norm3[i]), w_gate[i].astype(BF16), w_ple[i].astype(BF16), vec(final_norm),
                 final=(i == depth - 1))
    return h
```

```python
import functools

import jax
import jax.numpy as jnp
from jax import lax
from jax.experimental import pallas as pl
from jax.experimental.pallas import tpu as pltpu

F32 = jnp.float32
BF16 = jnp.bfloat16

D_MODEL = 1024
HEAD_DIM = 64
POOL_WIDTH = 256
POOL_WINDOWS = (2, 4, 8, 16)
POOL_GC = POOL_WIDTH // len(POOL_WINDOWS)
ATTN_WIDTH = D_MODEL - POOL_WIDTH
DILATIONS = (1, 4, 16)
GROUP_WIDTH = ATTN_WIDTH // len(DILATIONS)
ROT_DIM = HEAD_DIM // 4
ROPE_THETA = 500000.0
BLK = 128
D_FF = 4 * D_MODEL
EPS = 1e-6
MASK_VALUE = -1e30

LANES = 128
POOL_HALO = 16
ROW_TILE = 512
ATTN_ROWS = 2048
ATTN_MIN_ROWS = 512
ATTN_UNITS = 16
STAT_SPLIT = HEAD_DIM // 2
FF_CHUNK = 1024
VMEM_LIMIT = 56 * 1024 * 1024


def _rms(x, g):
    return x * lax.rsqrt(jnp.mean(x * x, axis=-1, keepdims=True) + EPS) * g


def _const_spec(shape):
    zeros = (0,) * len(shape)
    return pl.BlockSpec(shape, lambda *_: zeros, pipeline_mode=pl.Buffered(1))


def _rotary(x, cos_t, sin_t, first_half):
    half = ROT_DIM // 2
    partner = jnp.where(first_half, pltpu.roll(x, LANES - half, 1), pltpu.roll(x, half, 1))
    return x * cos_t + partner * sin_t


def _inproj_kernel(x_ref, g_ref, w_ref, cos_ref, sin_ref, u_ref, *refs):
    qkv_refs, stage = refs[:-1], refs[-1]
    tm = x_ref.shape[1]
    n_grp = len(DILATIONS)
    hn = _rms(x_ref[0], g_ref[...]).astype(BF16)
    cos_t = cos_ref[0]
    sin_t = sin_ref[0]
    lane = lax.broadcasted_iota(jnp.int32, cos_t.shape, 1)
    first_half = (lane % HEAD_DIM) < (ROT_DIM // 2)

    u_ref[0] = jnp.dot(hn, w_ref[:, :POOL_WIDTH], preferred_element_type=F32)
    for j, out_ref in enumerate(qkv_refs):
        col = POOL_WIDTH + j * GROUP_WIDTH
        z = jnp.dot(hn, w_ref[:, col:col + GROUP_WIDTH], preferred_element_type=F32)
        halves = [z[:, c:c + LANES] for c in range(0, GROUP_WIDTH, LANES)]
        if j < 2 * n_grp:
            halves = [_rotary(z_h, cos_t, sin_t, first_half) for z_h in halves]
        if j < n_grp:
            halves = [z_h * HEAD_DIM ** -0.5 for z_h in halves]
        dil = DILATIONS[j % n_grp]
        if dil == 1:
            out_ref[0] = jnp.concatenate(halves, axis=-1).astype(out_ref.dtype)
            continue
        n = tm // dil
        for t, z_h in enumerate(halves):
            stage[t] = z_h
        for r in range(dil):
            for t in range(len(halves)):
                c = r * GROUP_WIDTH + t * LANES
                out_ref[0, :, c:c + LANES] = stage[t, pl.ds(r, n, stride=dil), :].astype(out_ref.dtype)


def _inproj(x, g, w_in, cos_t, sin_t):
    B, S, D = x.shape
    n_in = w_in.shape[1]
    tm = ROW_TILE
    row = lambda width: pl.BlockSpec((1, tm, width), lambda b, i: (b, i, 0))
    strided = lambda dil: pl.BlockSpec((1, tm // dil, dil * GROUP_WIDTH), lambda b, i: (b, i, 0))
    out_shape = [jax.ShapeDtypeStruct((B, S, POOL_WIDTH), F32)]
    out_shape += [jax.ShapeDtypeStruct((B, S // dil, dil * GROUP_WIDTH), BF16) for dil in DILATIONS] * 3
    return pl.pallas_call(
        _inproj_kernel,
        grid=(B, S // tm),
        in_specs=[row(D), _const_spec((1, D)), _const_spec((D, n_in)), row(LANES), row(LANES)],
        out_specs=[row(POOL_WIDTH)] + [strided(dil) for dil in DILATIONS] * 3,
        out_shape=out_shape,
        scratch_shapes=[pltpu.VMEM((GROUP_WIDTH // LANES, tm, LANES), F32)],
        compiler_params=pltpu.CompilerParams(
            dimension_semantics=("parallel", "parallel"), vmem_limit_bytes=VMEM_LIMIT),
        name="inproj",
    )(x, g, w_in, cos_t, sin_t)


def _attn_kernel(q_ref, k_ref, kh_ref, v_ref, vh_ref, bias_ref, o_ref, stat_ref, kbuf, vbuf):
    rows, cols = q_ref.shape[1], q_ref.shape[2]
    n_blk = rows // BLK
    first_chunk = pl.program_id(1) == 0

    kbuf[0:BLK] = kh_ref[0]
    kbuf[BLK:] = k_ref[0]
    vbuf[0:BLK] = vh_ref[0]
    vbuf[BLK:] = v_ref[0]

    lane = lax.broadcasted_iota(jnp.int32, (BLK, LANES), 1)
    head0 = lane < HEAD_DIM
    stat_is_max = (lane % HEAD_DIM) < STAT_SPLIT
    ones = jnp.ones((2 * BLK, LANES), BF16)

    def unit(i, c):
        r0 = pl.multiple_of(i * BLK, BLK)
        q = q_ref[0, pl.ds(r0, BLK), c:c + LANES]
        kwin = kbuf[pl.ds(r0, 2 * BLK), c:c + LANES]
        vwin = vbuf[pl.ds(r0, 2 * BLK), c:c + LANES]
        zero = jnp.zeros_like(q)
        q2 = jnp.concatenate([jnp.where(head0, q, zero), jnp.where(head0, zero, q)], axis=0)
        s = lax.dot_general(q2, kwin, (((1,), (1,)), ((), ())), preferred_element_type=F32)
        no_history = jnp.logical_and(first_chunk, i == 0).astype(jnp.int32)
        s = s + bias_ref[no_history]
        m = jnp.max(s, axis=-1, keepdims=True)
        e = jnp.exp(s - m).astype(BF16)
        pv = jnp.dot(e, jnp.concatenate([vwin, ones], axis=1), preferred_element_type=F32)
        o_ref[0, pl.ds(r0, BLK), c:c + LANES] = jnp.where(
            head0, pv[:BLK, :LANES], pv[BLK:, :LANES]).astype(o_ref.dtype)
        m_both = jnp.where(head0, jnp.broadcast_to(m[:BLK], (BLK, LANES)), jnp.broadcast_to(m[BLK:], (BLK, LANES)))
        l_both = jnp.where(head0, pv[:BLK, LANES:], pv[BLK:, LANES:])
        stat_ref[0, pl.ds(r0, BLK), c:c + LANES] = jnp.where(stat_is_max, m_both, l_both)

    tiles = cols // LANES
    tiles_per_body = min(tiles, ATTN_UNITS)
    blocks_per_body = ATTN_UNITS // tiles_per_body
    for c0 in range(0, tiles, tiles_per_body):
        def body(ib, carry, c0=c0):
            for bi in range(blocks_per_body):
                for ci in range(tiles_per_body):
                    unit(ib * blocks_per_body + bi, (c0 + ci) * LANES)
            return carry
        lax.fori_loop(0, n_blk // blocks_per_body, body, 0)


def _attention(q, k, v, bias, dil):
    B, L, width = q.shape
    W = width // dil
    rows = min(L, max(ATTN_ROWS // dil, ATTN_MIN_ROWS))
    cols = (ATTN_ROWS // rows) * W
    hist_blocks = rows // BLK
    main = pl.BlockSpec((1, rows, cols), lambda b, n, r: (b, n, r))
    hist = pl.BlockSpec((1, BLK, cols), lambda b, n, r: (b, jnp.maximum(n * hist_blocks - 1, 0), r))
    return pl.pallas_call(
        _attn_kernel,
        grid=(B, L // rows, dil * W // cols),
        in_specs=[main, main, hist, main, hist, _const_spec(bias.shape)],
        out_specs=[main, main],
        out_shape=[jax.ShapeDtypeStruct((B, L, dil * W), BF16), jax.ShapeDtypeStruct((B, L, dil * W), F32)],
        scratch_shapes=[pltpu.VMEM((BLK + rows, cols), BF16), pltpu.VMEM((BLK + rows, cols), BF16)],
        compiler_params=pltpu.CompilerParams(
            dimension_semantics=("parallel", "arbitrary", "parallel"), vmem_limit_bytes=VMEM_LIMIT),
        name=f"attn_d{dil}",
    )(q, k, k, v, v, bias)


def _band_bias():
    a = jnp.arange(2 * BLK)[:, None] % BLK
    c = jnp.arange(2 * BLK)[None, :]
    band = (c >= a) & (c <= a + BLK)
    both = jnp.stack([band, band & (c >= BLK)])
    return jnp.where(both, 0.0, MASK_VALUE).astype(F32)


def _natural_order(ref, dil, stage):
    if dil == 1:
        return ref[0].astype(F32)
    n = ref.shape[1]
    tiles = GROUP_WIDTH // LANES
    for r in range(dil):
        for t in range(tiles):
            c = r * GROUP_WIDTH + t * LANES
            stage[t, pl.ds(r, n, stride=dil), :] = ref[0, :, c:c + LANES].astype(F32)
    return jnp.concatenate([stage[t] for t in range(tiles)], axis=-1)


def _mix_kernel(h_ref, u_ref, uh_ref, o1_ref, o4_ref, o16_ref, l1_ref, l4_ref, l16_ref, p_ref,
                wpool_ref, pscale_ref, wout_ref, g2_ref, wup_ref, wdown_ref, g3_ref, wgate_ref, wple_ref,
                gf_ref, out_ref, *stages, final):
    tm = h_ref.shape[1]
    first_tile = pl.program_id(1) == 0

    hist = jnp.where(first_tile, 0.0, uh_ref[0])
    ub = jnp.concatenate([hist, u_ref[0]], axis=0)
    s2 = ub + pltpu.roll(ub, 1, 0)
    s4 = s2 + pltpu.roll(s2, 2, 0)
    s8 = s4 + pltpu.roll(s4, 4, 0)
    s16 = s8 + pltpu.roll(s8, 8, 0)
    t = lax.broadcasted_iota(jnp.int32, (tm, LANES), 0) + pl.program_id(1) * tm + 1
    lane = lax.broadcasted_iota(jnp.int32, (tm, LANES), 1)
    low = lane < POOL_GC
    cnt = lambda w: jnp.minimum(t, w).astype(F32)
    tile = lambda s, c: s[POOL_HALO:, c:c + LANES]
    y_lo = jnp.where(low, tile(s2, 0) / cnt(2), tile(s4, 0) / cnt(4)) - tile(ub, 0)
    y_hi = jnp.where(low, tile(s8, LANES) / cnt(8), tile(s16, LANES) / cnt(16)) - tile(ub, LANES)
    y = jnp.concatenate([y_lo, y_hi], axis=-1).astype(BF16)
    pool = jnp.dot(y, wpool_ref[...], preferred_element_type=F32) * pscale_ref[...]

    o_refs, stat_refs = (o1_ref, o4_ref, o16_ref), (l1_ref, l4_ref, l16_ref)
    stage = iter(stages)
    pv = [_natural_order(r, d, next(stage) if d > 1 else None) for r, d in zip(o_refs, DILATIONS)]
    stat = [_natural_order(r, d, next(stage) if d > 1 else None) for r, d in zip(stat_refs, DILATIONS)]
    lane2 = lax.broadcasted_iota(jnp.int32, (tm, GROUP_WIDTH), 1)
    is_max = (lane2 % HEAD_DIM) < STAT_SPLIT
    m = [jnp.where(is_max, s, pltpu.roll(s, STAT_SPLIT, 1)) for s in stat]
    l = [jnp.where(is_max, pltpu.roll(s, GROUP_WIDTH - STAT_SPLIT, 1), s) for s in stat]
    mx = jnp.maximum(jnp.maximum(m[0], m[1]), m[2])
    e = [jnp.exp(m_g - mx) for m_g in m]
    inv = 1.0 / (l[0] * e[0] + l[1] * e[1] + l[2] * e[2])
    mix = jnp.concatenate(
        [pool.astype(BF16)] + [(pv_g * (e_g * inv)).astype(BF16) for pv_g, e_g in zip(pv, e)], axis=-1)
    h = h_ref[0] + jnp.dot(mix, wout_ref[...], preferred_element_type=F32)

    hn = _rms(h, g2_ref[...]).astype(BF16)
    acc = h
    for c in range(0, D_FF, FF_CHUNK):
        up = jnp.dot(hn, wup_ref[:, c:c + FF_CHUNK], preferred_element_type=F32)
        act = jnp.square(jnp.maximum(up, 0.0)).astype(BF16)
        acc = acc + jnp.dot(act, wdown_ref[c:c + FF_CHUNK, :], preferred_element_type=F32)
    h = acc

    hn = _rms(h, g3_ref[...]).astype(BF16)
    gate = 1.0 / (1.0 + jnp.exp(-jnp.dot(hn, wgate_ref[...], preferred_element_type=F32)))
    ple = jnp.dot(p_ref[0].astype(BF16), wple_ref[...], preferred_element_type=F32)
    h = h + gate * ple
    if final:
        h = _rms(h, gf_ref[...])
    out_ref[0] = h


def _mix(h, u, o, lse, p, wpool, pscale, wout, g2, wup, wdown, g3, wgate, wple, gf, final):
    B, S, D = h.shape
    tm = ROW_TILE
    row = lambda width: pl.BlockSpec((1, tm, width), lambda b, i: (b, i, 0))
    halo = pl.BlockSpec((1, POOL_HALO, POOL_WIDTH),
                        lambda b, i: (b, jnp.maximum(i * (tm // POOL_HALO) - 1, 0), 0))
    strided = [pl.BlockSpec((1, tm // dil, dil * GROUP_WIDTH), lambda b, i: (b, i, 0)) for dil in DILATIONS]
    weights = [wpool, pscale, wout, g2, wup, wdown, g3, wgate, wple, gf]
    return pl.pallas_call(
        functools.partial(_mix_kernel, final=final),
        grid=(B, S // tm),
        in_specs=[row(D), row(POOL_WIDTH), halo] + strided * 2 + [row(p.shape[-1])]
                 + [_const_spec(w.shape) for w in weights],
        out_specs=row(D),
        out_shape=jax.ShapeDtypeStruct((B, S, D), F32),
        scratch_shapes=[pltpu.VMEM((GROUP_WIDTH // LANES, tm, LANES), F32)
                        for dil in DILATIONS * 2 if dil > 1],
        compiler_params=pltpu.CompilerParams(
            dimension_semantics=("parallel", "arbitrary"), vmem_limit_bytes=VMEM_LIMIT),
        name="mix_final" if final else "mix",
    )(h, u, u, *o, *lse, p, *weights)


def _rotary_tables(positions):
    half = ROT_DIM // 2
    inv_freq = ROPE_THETA ** (-jnp.arange(0, ROT_DIM, 2, dtype=F32) / ROT_DIM)
    ang = positions.astype(F32)[..., None] * inv_freq
    cos, sin = jnp.cos(ang), jnp.sin(ang)
    rest = positions.shape + (HEAD_DIM - ROT_DIM,)
    cos_h = jnp.concatenate([cos, cos, jnp.ones(rest, F32)], axis=-1)
    sin_h = jnp.concatenate([-sin, sin, jnp.zeros(rest, F32)], axis=-1)
    reps = LANES // HEAD_DIM
    return jnp.tile(cos_h, (1, 1, reps)), jnp.tile(sin_h, (1, 1, reps))


def _block_diag(w):
    g, c, _ = w.shape
    eye = jnp.eye(g, dtype=w.dtype)
    return (w[:, :, None, :] * eye[:, None, :, None]).reshape(g * c, g * c)


def kernel(x, p, positions, norm1, w_in, pool_w, pool_scale, w_out, norm2, w_up, w_down, norm3, w_gate,
           w_ple, final_norm):
    depth = w_in.shape[0]
    n_grp = len(DILATIONS)
    cos_t, sin_t = _rotary_tables(positions)
    bias = _band_bias()
    vec = lambda a: a.reshape(1, -1)
    h = x
    for i in range(depth):
        u, *qkv = _inproj(h, vec(norm1[i]), w_in[i].astype(BF16), cos_t, sin_t)
        o, lse = [], []
        for g, dil in enumerate(DILATIONS):
            og, lg = _attention(qkv[g], qkv[n_grp + g], qkv[2 * n_grp + g], bias, dil)
            o.append(og)
            lse.append(lg)
        h = _mix(h, u, o, lse, p[i], _block_diag(pool_w[i]).astype(BF16), vec(pool_scale[i]),
                 w_out[i].astype(BF16), vec(norm2[i]), w_up[i].astype(BF16), w_down[i].astype(BF16),
                 vec(norm3[i]), w_gate[i].astype(BF16), w_ple[i].astype(BF16), vec(final_norm),
                 final=(i == depth - 1))
    return h
```

```python
import functools

import numpy as np
import jax
import jax.numpy as jnp
from jax import lax
from jax.experimental import pallas as pl
from jax.experimental.pallas import tpu as pltpu

F32 = jnp.float32
BF16 = jnp.bfloat16

D_MODEL = 1024
HEAD_DIM = 64
POOL_WIDTH = 256
POOL_WINDOWS = (2, 4, 8, 16)
POOL_GC = POOL_WIDTH // len(POOL_WINDOWS)
ATTN_WIDTH = D_MODEL - POOL_WIDTH
DILATIONS = (1, 4, 16)
GROUP_WIDTH = ATTN_WIDTH // len(DILATIONS)
ROT_DIM = HEAD_DIM // 4
ROPE_THETA = 500000.0
BLK = 128
D_FF = 4 * D_MODEL
EPS = 1e-6
MASK_VALUE = -1e30

LANES = 128
POOL_HALO = 16
ROW_TILE = 512
ATTN_ROWS = 2048
ATTN_MIN_ROWS = 512
ATTN_UNITS = 16
STAT_SPLIT = HEAD_DIM // 2
FF_CHUNK = 1024
GATE_CHUNK = 256
VMEM_LIMIT = 56 * 1024 * 1024


def _rms(x, g):
    return x * lax.rsqrt(jnp.mean(x * x, axis=-1, keepdims=True) + EPS) * g


def _const_spec(shape):
    zeros = (0,) * len(shape)
    return pl.BlockSpec(shape, lambda *_: zeros, pipeline_mode=pl.Buffered(1))


_HALF = ROT_DIM // 2
_QK_TILE_PERM = np.concatenate([
    np.arange(0, _HALF), np.arange(HEAD_DIM, HEAD_DIM + _HALF), np.arange(ROT_DIM, HEAD_DIM),
    np.arange(_HALF, ROT_DIM), np.arange(HEAD_DIM + _HALF, HEAD_DIM + ROT_DIM),
    np.arange(HEAD_DIM + ROT_DIM, LANES)])
_QKV_ORDER = ((0, 2), (0, 0), (1, 2), (0, 1), (2, 2), (1, 0), (1, 1), (2, 0), (2, 1))


def _w_in_blocks(w):
    runs = np.split(_QK_TILE_PERM, np.flatnonzero(np.diff(_QK_TILE_PERM) != 1) + 1)
    parts = []
    for kind, g in _QKV_ORDER:
        for t in range(0, GROUP_WIDTH, LANES):
            base = POOL_WIDTH + kind * ATTN_WIDTH + g * GROUP_WIDTH + t
            if kind == 2:
                parts.append(w[:, base:base + LANES])
                continue
            scale = HEAD_DIM ** -0.5 if kind == 0 else 1.0
            parts += [w[:, base + run[0]:base + run[-1] + 1] * scale for run in runs]
    parts.append(w[:, :POOL_WIDTH])
    return jnp.concatenate(parts, axis=1).astype(BF16)


def _head_a_lanes(lane):
    return (lane < _HALF) | ((lane >= ROT_DIM) & (lane < HEAD_DIM + _HALF))


def _inproj_kernel(x_ref, g_ref, w_ref, cos_ref, sin_ref, u_ref, *refs):
    n_grp = len(DILATIONS)
    qkv_refs, stages = refs[:3 * n_grp], refs[3 * n_grp:]
    tm = x_ref.shape[1]
    hn = _rms(x_ref[0], g_ref[...]).astype(BF16)
    cos_t = cos_ref[0]
    sin_t = sin_ref[0]

    for b, (kind, g) in enumerate(_QKV_ORDER):
        out_ref = qkv_refs[kind * n_grp + g]
        z = jnp.dot(hn, w_ref[:, b * GROUP_WIDTH:(b + 1) * GROUP_WIDTH], preferred_element_type=F32)
        halves = [z[:, c:c + LANES] for c in range(0, GROUP_WIDTH, LANES)]
        if kind < 2:
            halves = [z_h * cos_t + pltpu.roll(z_h, LANES // 2, 1) * sin_t for z_h in halves]
        dil = DILATIONS[g]
        if dil == 1:
            out_ref[0] = jnp.concatenate(halves, axis=-1).astype(out_ref.dtype)
            continue
        n = tm // dil
        stage = stages[b % len(stages)]
        for t, z_h in enumerate(halves):
            stage[t] = z_h
        for r in range(dil):
            for t in range(len(halves)):
                c = r * GROUP_WIDTH + t * LANES
                out_ref[0, :, c:c + LANES] = stage[t, pl.ds(r, n, stride=dil), :].astype(out_ref.dtype)
    u_ref[0] = jnp.dot(hn, w_ref[:, len(_QKV_ORDER) * GROUP_WIDTH:], preferred_element_type=F32)


def _inproj(x, g, w_in, cos_t, sin_t):
    B, S, D = x.shape
    n_in = w_in.shape[1]
    tm = ROW_TILE
    row = lambda width: pl.BlockSpec((1, tm, width), lambda b, i: (b, i, 0))
    strided = lambda dil: pl.BlockSpec((1, tm // dil, dil * GROUP_WIDTH), lambda b, i: (b, i, 0))
    out_shape = [jax.ShapeDtypeStruct((B, S, POOL_WIDTH), F32)]
    out_shape += [jax.ShapeDtypeStruct((B, S // dil, dil * GROUP_WIDTH), BF16) for dil in DILATIONS] * 3
    return pl.pallas_call(
        _inproj_kernel,
        grid=(B, S // tm),
        in_specs=[row(D), _const_spec((1, D)), _const_spec((D, n_in)), row(LANES), row(LANES)],
        out_specs=[row(POOL_WIDTH)] + [strided(dil) for dil in DILATIONS] * 3,
        out_shape=out_shape,
        scratch_shapes=[pltpu.VMEM((GROUP_WIDTH // LANES, tm, LANES), F32)] * 2,
        compiler_params=pltpu.CompilerParams(
            dimension_semantics=("parallel", "parallel"), vmem_limit_bytes=VMEM_LIMIT),
        name="inproj",
    )(x, g, w_in, cos_t, sin_t)


def _attn_kernel(q_ref, k_ref, kh_ref, v_ref, vh_ref, bias_ref, o_ref, stat_ref, kbuf, vbuf):
    rows, cols = q_ref.shape[1], q_ref.shape[2]
    n_blk = rows // BLK
    first_chunk = pl.program_id(1) == 0

    kbuf[0:BLK] = kh_ref[0]
    kbuf[BLK:] = k_ref[0]
    vbuf[0:BLK] = vh_ref[0]
    vbuf[BLK:] = v_ref[0]

    lane = lax.broadcasted_iota(jnp.int32, (BLK, LANES), 1)
    head0 = lane < HEAD_DIM
    head0_qk = _head_a_lanes(lane)
    stat_is_max = (lane % HEAD_DIM) < STAT_SPLIT
    ones = jnp.ones((2 * BLK, LANES), BF16)

    def unit(i, c):
        r0 = pl.multiple_of(i * BLK, BLK)
        q = q_ref[0, pl.ds(r0, BLK), c:c + LANES]
        kwin = kbuf[pl.ds(r0, 2 * BLK), c:c + LANES]
        vwin = vbuf[pl.ds(r0, 2 * BLK), c:c + LANES]
        zero = jnp.zeros_like(q)
        q2 = jnp.concatenate([jnp.where(head0_qk, q, zero), jnp.where(head0_qk, zero, q)], axis=0)
        s = lax.dot_general(q2, kwin, (((1,), (1,)), ((), ())), preferred_element_type=F32)
        no_history = jnp.logical_and(first_chunk, i == 0).astype(jnp.int32)
        s = s + bias_ref[no_history]
        m = jnp.max(s, axis=-1, keepdims=True)
        e = jnp.exp(s - m).astype(BF16)
        pv = jnp.dot(e, jnp.concatenate([vwin, ones], axis=1), preferred_element_type=F32)
        o_ref[0, pl.ds(r0, BLK), c:c + LANES] = jnp.where(
            head0, pv[:BLK, :LANES], pv[BLK:, :LANES]).astype(o_ref.dtype)
        m_both = jnp.where(head0, jnp.broadcast_to(m[:BLK], (BLK, LANES)), jnp.broadcast_to(m[BLK:], (BLK, LANES)))
        l_both = jnp.where(head0, pv[:BLK, LANES:], pv[BLK:, LANES:])
        stat_ref[0, pl.ds(r0, BLK), c:c + LANES] = jnp.where(stat_is_max, m_both, l_both)

    tiles = cols // LANES
    tiles_per_body = min(tiles, ATTN_UNITS)
    blocks_per_body = ATTN_UNITS // tiles_per_body
    for c0 in range(0, tiles, tiles_per_body):
        def body(ib, carry, c0=c0):
            for bi in range(blocks_per_body):
                for ci in range(tiles_per_body):
                    unit(ib * blocks_per_body + bi, (c0 + ci) * LANES)
            return carry
        lax.fori_loop(0, n_blk // blocks_per_body, body, 0)


def _attention(q, k, v, bias, dil):
    B, L, width = q.shape
    W = width // dil
    rows = min(L, max(ATTN_ROWS // dil, ATTN_MIN_ROWS))
    cols = (ATTN_ROWS // rows) * W
    hist_blocks = rows // BLK
    main = pl.BlockSpec((1, rows, cols), lambda b, n, r: (b, n, r))
    hist = pl.BlockSpec((1, BLK, cols), lambda b, n, r: (b, jnp.maximum(n * hist_blocks - 1, 0), r))
    return pl.pallas_call(
        _attn_kernel,
        grid=(B, L // rows, dil * W // cols),
        in_specs=[main, main, hist, main, hist, _const_spec(bias.shape)],
        out_specs=[main, main],
        out_shape=[jax.ShapeDtypeStruct((B, L, dil * W), BF16), jax.ShapeDtypeStruct((B, L, dil * W), F32)],
        scratch_shapes=[pltpu.VMEM((BLK + rows, cols), BF16), pltpu.VMEM((BLK + rows, cols), BF16)],
        compiler_params=pltpu.CompilerParams(
            dimension_semantics=("parallel", "arbitrary", "parallel"), vmem_limit_bytes=VMEM_LIMIT),
        name=f"attn_d{dil}",
    )(q, k, k, v, v, bias)


def _band_bias():
    a = jnp.arange(2 * BLK)[:, None] % BLK
    c = jnp.arange(2 * BLK)[None, :]
    band = (c >= a) & (c <= a + BLK)
    both = jnp.stack([band, band & (c >= BLK)])
    return jnp.where(both, 0.0, MASK_VALUE).astype(F32)


def _natural_order(ref, dil, stage):
    if dil == 1:
        return ref[0].astype(F32)
    n = ref.shape[1]
    tiles = GROUP_WIDTH // LANES
    for r in range(dil):
        for t in range(tiles):
            c = r * GROUP_WIDTH + t * LANES
            stage[t, pl.ds(r, n, stride=dil), :] = ref[0, :, c:c + LANES].astype(F32)
    return jnp.concatenate([stage[t] for t in range(tiles)], axis=-1)


def _pool_mixer(tile_in_seq, u_ref, uh_ref, wpool_ref, pscale_ref):
    tm = u_ref.shape[1]
    hist = jnp.where(tile_in_seq == 0, 0.0, uh_ref[0])
    ub = jnp.concatenate([hist, u_ref[0]], axis=0)
    s2 = ub + pltpu.roll(ub, 1, 0)
    s4 = s2 + pltpu.roll(s2, 2, 0)
    s8 = s4 + pltpu.roll(s4, 4, 0)
    s16 = s8 + pltpu.roll(s8, 8, 0)
    t = lax.broadcasted_iota(jnp.int32, (tm, LANES), 0) + tile_in_seq * tm + 1
    lane = lax.broadcasted_iota(jnp.int32, (tm, LANES), 1)
    low = lane < POOL_GC
    cnt = lambda w: jnp.minimum(t, w).astype(F32)
    tile = lambda s, c: s[POOL_HALO:, c:c + LANES]
    y_lo = jnp.where(low, tile(s2, 0) / cnt(2), tile(s4, 0) / cnt(4)) - tile(ub, 0)
    y_hi = jnp.where(low, tile(s8, LANES) / cnt(8), tile(s16, LANES) / cnt(16)) - tile(ub, LANES)
    y = jnp.concatenate([y_lo, y_hi], axis=-1).astype(BF16)
    return jnp.dot(y, wpool_ref[...], preferred_element_type=F32) * pscale_ref[...]


def _group_scales(stat_refs, stages):
    stat = [_natural_order(r, d, stage) for r, d, stage in zip(stat_refs, DILATIONS, stages)]
    lane = lax.broadcasted_iota(jnp.int32, stat[0].shape, 1)
    is_max = (lane % HEAD_DIM) < STAT_SPLIT
    m = [jnp.where(is_max, s, pltpu.roll(s, STAT_SPLIT, 1)) for s in stat]
    l = [jnp.where(is_max, pltpu.roll(s, GROUP_WIDTH - STAT_SPLIT, 1), s) for s in stat]
    mx = jnp.maximum(jnp.maximum(m[0], m[1]), m[2])
    e = [jnp.exp(m_g - mx) for m_g in m]
    inv = 1.0 / (l[0] * e[0] + l[1] * e[1] + l[2] * e[2])
    return [e_g * inv for e_g in e]


def _mix_kernel(h_ref, u_ref, uh_ref, o1_ref, o4_ref, o16_ref, l1_ref, l4_ref, l16_ref, p_ref,
                wpool_ref, pscale_ref, wout_ref, g2_ref, wup_ref, wdown_ref, g3_ref, wgate_ref, wple_ref,
                gf_ref, out_ref, carry, *stages, final, tiles_per_seq, n_tiles):
    s = pl.program_id(0)
    D = h_ref.shape[2]
    tile_in_seq = jnp.minimum(s, n_tiles - 1) % tiles_per_seq
    o_refs, stat_refs = (o1_ref, o4_ref, o16_ref), (l1_ref, l4_ref, l16_ref)
    stage = iter(stages)
    o_stages = [next(stage) if d > 1 else None for d in DILATIONS]
    stat_stages = [next(stage) if d > 1 else None for d in DILATIONS]

    @pl.when(s == 0)
    def _():
        carry[...] = jnp.zeros(carry.shape, carry.dtype)

    h = h_ref[0] + carry[...]
    ple = jnp.dot(p_ref[0, 0].astype(BF16), wple_ref[...], preferred_element_type=F32)
    scales = _group_scales(stat_refs, stat_stages)

    def next_product(piece):
        if piece < len(DILATIONS):
            val = _natural_order(o_refs[piece], DILATIONS[piece], o_stages[piece]) * scales[piece]
            row = POOL_WIDTH + piece * GROUP_WIDTH
        else:
            val = _pool_mixer(tile_in_seq, u_ref, uh_ref, wpool_ref, pscale_ref)
            row = 0
        return jnp.dot(val.astype(BF16), wout_ref[row:row + GROUP_WIDTH, :], preferred_element_type=F32)

    hn = _rms(h, g2_ref[...]).astype(BF16)
    acc = h
    nxt = None
    for piece, c in enumerate(range(0, D_FF, FF_CHUNK)):
        up = jnp.dot(hn, wup_ref[:, c:c + FF_CHUNK], preferred_element_type=F32)
        act = jnp.square(jnp.maximum(up, 0.0)).astype(BF16)
        acc = acc + jnp.dot(act, wdown_ref[c:c + FF_CHUNK, :], preferred_element_type=F32)
        if piece > 0:
            prod = next_product(piece - 1)
            nxt = prod if nxt is None else nxt + prod
    h = acc

    hn = _rms(h, g3_ref[...]).astype(BF16)
    pieces = []
    for c in range(0, D, GATE_CHUNK):
        z = jnp.dot(hn, wgate_ref[:, c:c + GATE_CHUNK], preferred_element_type=F32)
        gate = 0.5 + 0.5 * jnp.tanh(0.5 * z)
        pieces.append(h[:, c:c + GATE_CHUNK] + gate * ple[:, c:c + GATE_CHUNK])
    carry[...] = nxt + next_product(len(DILATIONS))
    if final:
        ssq = sum(jnp.sum(x * x, axis=-1, keepdims=True) for x in pieces)
        inv = lax.rsqrt(ssq / D + EPS)
        pieces = [x * inv * gf_ref[:, c:c + GATE_CHUNK] for x, c in zip(pieces, range(0, D, GATE_CHUNK))]
    for x, c in zip(pieces, range(0, D, GATE_CHUNK)):
        out_ref[0, :, c:c + GATE_CHUNK] = x


def _mix(h, u, o, stats, p, layer, wpool, pscale, wout, g2, wup, wdown, g3, wgate, wple, gf, final):
    B, S, D = h.shape
    tm = ROW_TILE
    nt = S // tm
    n_tiles = B * nt
    ahead = lambda s: jnp.minimum(s, n_tiles - 1)
    behind = lambda s: jnp.maximum(s - 1, 0)
    row_b = lambda width: pl.BlockSpec((1, tm, width), lambda s: (behind(s) // nt, behind(s) % nt, 0))
    row_a = lambda width: pl.BlockSpec((1, tm, width), lambda s: (ahead(s) // nt, ahead(s) % nt, 0))
    halo = pl.BlockSpec(
        (1, POOL_HALO, POOL_WIDTH),
        lambda s: (ahead(s) // nt, jnp.maximum(ahead(s) % nt * (tm // POOL_HALO) - 1, 0), 0))
    strided = [pl.BlockSpec((1, tm // dil, dil * GROUP_WIDTH), lambda s: (ahead(s) // nt, ahead(s) % nt, 0))
               for dil in DILATIONS]
    p_spec = pl.BlockSpec((1, 1, tm, p.shape[-1]), lambda s: (layer, behind(s) // nt, behind(s) % nt, 0))
    weights = [wpool, pscale, wout, g2, wup, wdown, g3, wgate, wple, gf]
    return pl.pallas_call(
        functools.partial(_mix_kernel, final=final, tiles_per_seq=nt, n_tiles=n_tiles),
        grid=(n_tiles + 1,),
        in_specs=[row_b(D), row_a(POOL_WIDTH), halo] + strided * 2 + [p_spec]
                 + [_const_spec(w.shape) for w in weights],
        out_specs=row_b(D),
        out_shape=jax.ShapeDtypeStruct((B, S, D), F32),
        scratch_shapes=[pltpu.VMEM((tm, D), F32)]
                       + [pltpu.VMEM((GROUP_WIDTH // LANES, tm, LANES), F32) for dil in DILATIONS * 2 if dil > 1],
        compiler_params=pltpu.CompilerParams(
            dimension_semantics=("arbitrary",), vmem_limit_bytes=VMEM_LIMIT),
        name="mix_final" if final else "mix",
    )(h, u, u, *o, *stats, p, *weights)


def _rotary_tables(positions):
    inv_freq = ROPE_THETA ** (-jnp.arange(0, ROT_DIM, 2, dtype=F32) / ROT_DIM)
    ang = positions.astype(F32)[..., None] * inv_freq
    cos, sin = jnp.cos(ang), jnp.sin(ang)
    rest = positions.shape + (HEAD_DIM - ROT_DIM,)
    ones, zeros = jnp.ones(rest, F32), jnp.zeros(rest, F32)
    cos_t = jnp.concatenate([cos, cos, ones, cos, cos, ones], axis=-1)
    sin_t = jnp.concatenate([-sin, -sin, zeros, sin, sin, zeros], axis=-1)
    return cos_t, sin_t


def _block_diag(w):
    g, c, _ = w.shape
    eye = jnp.eye(g, dtype=w.dtype)
    return (w[:, :, None, :] * eye[:, None, :, None]).reshape(g * c, g * c)


def kernel(x, p, positions, norm1, w_in, pool_w, pool_scale, w_out, norm2, w_up, w_down, norm3, w_gate,
           w_ple, final_norm):
    depth = w_in.shape[0]
    n_grp = len(DILATIONS)
    cos_t, sin_t = _rotary_tables(positions)
    bias = _band_bias()
    vec = lambda a: a.reshape(1, -1)
    h = x
    for i in range(depth):
        u, *qkv = _inproj(h, vec(norm1[i]), _w_in_blocks(w_in[i]), cos_t, sin_t)
        o, stats = [], []
        for g, dil in enumerate(DILATIONS):
            og, sg = _attention(qkv[g], qkv[n_grp + g], qkv[2 * n_grp + g], bias, dil)
            o.append(og)
            stats.append(sg)
        h = _mix(h, u, o, stats, p, i, _block_diag(pool_w[i]).astype(BF16), vec(pool_scale[i]),
                 w_out[i].astype(BF16), vec(norm2[i]), w_up[i].astype(BF16), w_down[i].astype(BF16),
                 vec(norm3[i]), w_gate[i].astype(BF16), w_ple[i].astype(BF16), vec(final_norm),
                 final=(i == depth - 1))
    return h
```

```python
import functools

import numpy as np
import jax
import jax.numpy as jnp
from jax import lax
from jax.experimental import pallas as pl
from jax.experimental.pallas import tpu as pltpu

F32 = jnp.float32
BF16 = jnp.bfloat16

D_MODEL = 1024
HEAD_DIM = 64
POOL_WIDTH = 256
POOL_WINDOWS = (2, 4, 8, 16)
POOL_GC = POOL_WIDTH // len(POOL_WINDOWS)
ATTN_WIDTH = D_MODEL - POOL_WIDTH
DILATIONS = (1, 4, 16)
GROUP_WIDTH = ATTN_WIDTH // len(DILATIONS)
ROT_DIM = HEAD_DIM // 4
ROPE_THETA = 500000.0
BLK = 128
D_FF = 4 * D_MODEL
EPS = 1e-6
MASK_VALUE = -1e30

LANES = 128
POOL_HALO = 16
ROW_TILE = 512
ATTN_ROWS = 2048
ATTN_MIN_ROWS = 512
ATTN_UNITS = 16
STAT_SPLIT = HEAD_DIM // 2
FF_CHUNK = 1024
GATE_CHUNK = 256
VMEM_LIMIT = 56 * 1024 * 1024


def _rms(x, g):
    return x * lax.rsqrt(jnp.mean(x * x, axis=-1, keepdims=True) + EPS) * g


def _const_spec(shape):
    zeros = (0,) * len(shape)
    return pl.BlockSpec(shape, lambda *_: zeros, pipeline_mode=pl.Buffered(1))


_HALF = ROT_DIM // 2
_QK_TILE_PERM = np.concatenate([
    np.arange(0, _HALF), np.arange(HEAD_DIM, HEAD_DIM + _HALF), np.arange(ROT_DIM, HEAD_DIM),
    np.arange(_HALF, ROT_DIM), np.arange(HEAD_DIM + _HALF, HEAD_DIM + ROT_DIM),
    np.arange(HEAD_DIM + ROT_DIM, LANES)])
_QKV_ORDER = ((0, 2), (0, 0), (1, 2), (0, 1), (2, 2), (1, 0), (1, 1), (2, 0), (2, 1))


def _w_in_prepared(w):
    col = np.arange(w.shape[1])
    is_qk = (col >= POOL_WIDTH) & (col < POOL_WIDTH + 2 * ATTN_WIDTH)
    src = col.copy()
    tiles = col[is_qk].reshape(-1, LANES)
    src[is_qk] = (tiles[:, :1] + _QK_TILE_PERM[None, :]).reshape(-1)
    shift = src - col
    scale = np.where((col >= POOL_WIDTH) & (col < POOL_WIDTH + ATTN_WIDTH), HEAD_DIM ** -0.5, 1.0)
    out = w
    for d in sorted(set(shift[shift != 0])):
        out = jnp.where(shift == d, jnp.roll(w, -int(d), axis=1), out)
    return (out * scale.astype(np.float32)).astype(BF16)


def _head_a_lanes(lane):
    return (lane < _HALF) | ((lane >= ROT_DIM) & (lane < HEAD_DIM + _HALF))


def _inproj_kernel(x_ref, g_ref, w_ref, cos_ref, sin_ref, u_ref, *refs):
    n_grp = len(DILATIONS)
    qkv_refs, stages = refs[:3 * n_grp], refs[3 * n_grp:]
    tm = x_ref.shape[1]
    hn = _rms(x_ref[0], g_ref[...]).astype(BF16)
    cos_t = cos_ref[0]
    sin_t = sin_ref[0]

    for b, (kind, g) in enumerate(_QKV_ORDER):
        out_ref = qkv_refs[kind * n_grp + g]
        col = POOL_WIDTH + kind * ATTN_WIDTH + g * GROUP_WIDTH
        z = jnp.dot(hn, w_ref[:, col:col + GROUP_WIDTH], preferred_element_type=F32)
        halves = [z[:, c:c + LANES] for c in range(0, GROUP_WIDTH, LANES)]
        if kind < 2:
            halves = [z_h * cos_t + pltpu.roll(z_h, LANES // 2, 1) * sin_t for z_h in halves]
        dil = DILATIONS[g]
        if dil == 1:
            out_ref[0] = jnp.concatenate(halves, axis=-1).astype(out_ref.dtype)
            continue
        n = tm // dil
        stage = stages[b % len(stages)]
        for t, z_h in enumerate(halves):
            stage[t] = z_h
        for r in range(dil):
            for t in range(len(halves)):
                c = r * GROUP_WIDTH + t * LANES
                out_ref[0, :, c:c + LANES] = stage[t, pl.ds(r, n, stride=dil), :].astype(out_ref.dtype)
    u_ref[0] = jnp.dot(hn, w_ref[:, :POOL_WIDTH], preferred_element_type=F32)


def _inproj(x, g, w_in, cos_t, sin_t):
    B, S, D = x.shape
    n_in = w_in.shape[1]
    tm = ROW_TILE
    row = lambda width: pl.BlockSpec((1, tm, width), lambda b, i: (b, i, 0))
    strided = lambda dil: pl.BlockSpec((1, tm // dil, dil * GROUP_WIDTH), lambda b, i: (b, i, 0))
    out_shape = [jax.ShapeDtypeStruct((B, S, POOL_WIDTH), F32)]
    out_shape += [jax.ShapeDtypeStruct((B, S // dil, dil * GROUP_WIDTH), BF16) for dil in DILATIONS] * 3
    return pl.pallas_call(
        _inproj_kernel,
        grid=(B, S // tm),
        in_specs=[row(D), _const_spec((1, D)), _const_spec((D, n_in)), row(LANES), row(LANES)],
        out_specs=[row(POOL_WIDTH)] + [strided(dil) for dil in DILATIONS] * 3,
        out_shape=out_shape,
        scratch_shapes=[pltpu.VMEM((GROUP_WIDTH // LANES, tm, LANES), F32)] * 2,
        compiler_params=pltpu.CompilerParams(
            dimension_semantics=("parallel", "parallel"), vmem_limit_bytes=VMEM_LIMIT),
        name="inproj",
    )(x, g, w_in, cos_t, sin_t)


def _attn_kernel(q_ref, k_ref, kh_ref, v_ref, vh_ref, bias_ref, o_ref, stat_ref, kbuf, vbuf):
    rows, cols = q_ref.shape[1], q_ref.shape[2]
    n_blk = rows // BLK
    first_chunk = pl.program_id(1) == 0

    kbuf[0:BLK] = kh_ref[0]
    kbuf[BLK:] = k_ref[0]
    vbuf[0:BLK] = vh_ref[0]
    vbuf[BLK:] = v_ref[0]

    lane = lax.broadcasted_iota(jnp.int32, (BLK, LANES), 1)
    head0 = lane < HEAD_DIM
    head0_qk = _head_a_lanes(lane)
    stat_is_max = (lane % HEAD_DIM) < STAT_SPLIT
    ones = jnp.ones((2 * BLK, LANES), BF16)

    def unit(i, c):
        r0 = pl.multiple_of(i * BLK, BLK)
        q = q_ref[0, pl.ds(r0, BLK), c:c + LANES]
        kwin = kbuf[pl.ds(r0, 2 * BLK), c:c + LANES]
        vwin = vbuf[pl.ds(r0, 2 * BLK), c:c + LANES]
        zero = jnp.zeros_like(q)
        q2 = jnp.concatenate([jnp.where(head0_qk, q, zero), jnp.where(head0_qk, zero, q)], axis=0)
        s = lax.dot_general(q2, kwin, (((1,), (1,)), ((), ())), preferred_element_type=F32)
        no_history = jnp.logical_and(first_chunk, i == 0).astype(jnp.int32)
        s = s + bias_ref[no_history]
        m = jnp.max(s, axis=-1, keepdims=True)
        e = jnp.exp(s - m).astype(BF16)
        pv = jnp.dot(e, jnp.concatenate([vwin, ones], axis=1), preferred_element_type=F32)
        o_ref[0, pl.ds(r0, BLK), c:c + LANES] = jnp.where(
            head0, pv[:BLK, :LANES], pv[BLK:, :LANES]).astype(o_ref.dtype)
        m_both = jnp.where(head0, jnp.broadcast_to(m[:BLK], (BLK, LANES)), jnp.broadcast_to(m[BLK:], (BLK, LANES)))
        l_both = jnp.where(head0, pv[:BLK, LANES:], pv[BLK:, LANES:])
        stat_ref[0, pl.ds(r0, BLK), c:c + LANES] = jnp.where(stat_is_max, m_both, l_both)

    tiles = cols // LANES
    tiles_per_body = min(tiles, ATTN_UNITS)
    blocks_per_body = ATTN_UNITS // tiles_per_body
    for c0 in range(0, tiles, tiles_per_body):
        def body(ib, carry, c0=c0):
            for bi in range(blocks_per_body):
                for ci in range(tiles_per_body):
                    unit(ib * blocks_per_body + bi, (c0 + ci) * LANES)
            return carry
        lax.fori_loop(0, n_blk // blocks_per_body, body, 0)


def _attention(q, k, v, bias, dil):
    B, L, width = q.shape
    W = width // dil
    rows = min(L, max(ATTN_ROWS // dil, ATTN_MIN_ROWS))
    cols = (ATTN_ROWS // rows) * W
    hist_blocks = rows // BLK
    main = pl.BlockSpec((1, rows, cols), lambda b, n, r: (b, n, r))
    hist = pl.BlockSpec((1, BLK, cols), lambda b, n, r: (b, jnp.maximum(n * hist_blocks - 1, 0), r))
    return pl.pallas_call(
        _attn_kernel,
        grid=(B, L // rows, dil * W // cols),
        in_specs=[main, main, hist, main, hist, _const_spec(bias.shape)],
        out_specs=[main, main],
        out_shape=[jax.ShapeDtypeStruct((B, L, dil * W), BF16), jax.ShapeDtypeStruct((B, L, dil * W), F32)],
        scratch_shapes=[pltpu.VMEM((BLK + rows, cols), BF16), pltpu.VMEM((BLK + rows, cols), BF16)],
        compiler_params=pltpu.CompilerParams(
            dimension_semantics=("parallel", "arbitrary", "parallel"), vmem_limit_bytes=VMEM_LIMIT),
        name=f"attn_d{dil}",
    )(q, k, k, v, v, bias)


def _band_bias():
    a = jnp.arange(2 * BLK)[:, None] % BLK
    c = jnp.arange(2 * BLK)[None, :]
    band = (c >= a) & (c <= a + BLK)
    both = jnp.stack([band, band & (c >= BLK)])
    return jnp.where(both, 0.0, MASK_VALUE).astype(F32)


def _natural_order(ref, dil, stage):
    if dil == 1:
        return ref[0].astype(F32)
    n = ref.shape[1]
    tiles = GROUP_WIDTH // LANES
    for r in range(dil):
        for t in range(tiles):
            c = r * GROUP_WIDTH + t * LANES
            stage[t, pl.ds(r, n, stride=dil), :] = ref[0, :, c:c + LANES].astype(F32)
    return jnp.concatenate([stage[t] for t in range(tiles)], axis=-1)


def _pool_mixer(tile_in_seq, u_ref, uh_ref, wpool_ref, pscale_ref):
    tm = u_ref.shape[1]
    hist = jnp.where(tile_in_seq == 0, 0.0, uh_ref[0])
    ub = jnp.concatenate([hist, u_ref[0]], axis=0)
    s2 = ub + pltpu.roll(ub, 1, 0)
    s4 = s2 + pltpu.roll(s2, 2, 0)
    s8 = s4 + pltpu.roll(s4, 4, 0)
    s16 = s8 + pltpu.roll(s8, 8, 0)
    t = lax.broadcasted_iota(jnp.int32, (tm, LANES), 0) + tile_in_seq * tm + 1
    lane = lax.broadcasted_iota(jnp.int32, (tm, LANES), 1)
    low = lane < POOL_GC
    cnt = lambda w: jnp.minimum(t, w).astype(F32)
    tile = lambda s, c: s[POOL_HALO:, c:c + LANES]
    y_lo = jnp.where(low, tile(s2, 0) / cnt(2), tile(s4, 0) / cnt(4)) - tile(ub, 0)
    y_hi = jnp.where(low, tile(s8, LANES) / cnt(8), tile(s16, LANES) / cnt(16)) - tile(ub, LANES)
    y = jnp.concatenate([y_lo, y_hi], axis=-1).astype(BF16)
    return jnp.dot(y, wpool_ref[...], preferred_element_type=F32) * pscale_ref[...]


def _group_scales(stat_refs, stages):
    stat = [_natural_order(r, d, stage) for r, d, stage in zip(stat_refs, DILATIONS, stages)]
    lane = lax.broadcasted_iota(jnp.int32, stat[0].shape, 1)
    is_max = (lane % HEAD_DIM) < STAT_SPLIT
    m = [jnp.where(is_max, s, pltpu.roll(s, STAT_SPLIT, 1)) for s in stat]
    l = [jnp.where(is_max, pltpu.roll(s, GROUP_WIDTH - STAT_SPLIT, 1), s) for s in stat]
    mx = jnp.maximum(jnp.maximum(m[0], m[1]), m[2])
    e = [jnp.exp(m_g - mx) for m_g in m]
    inv = 1.0 / (l[0] * e[0] + l[1] * e[1] + l[2] * e[2])
    return [e_g * inv for e_g in e]


def _mix_kernel(h_ref, u_ref, uh_ref, o1_ref, o4_ref, o16_ref, l1_ref, l4_ref, l16_ref, p_ref,
                wpool_ref, pscale_ref, wout_ref, g2_ref, wup_ref, wdown_ref, g3_ref, wgate_ref, wple_ref,
                gf_ref, out_ref, carry, *stages, final, tiles_per_seq, n_tiles):
    s = pl.program_id(0)
    D = h_ref.shape[2]
    tile_in_seq = jnp.minimum(s, n_tiles - 1) % tiles_per_seq
    o_refs, stat_refs = (o1_ref, o4_ref, o16_ref), (l1_ref, l4_ref, l16_ref)
    stage = iter(stages)
    o_stages = [next(stage) if d > 1 else None for d in DILATIONS]
    stat_stages = [next(stage) if d > 1 else None for d in DILATIONS]

    @pl.when(s == 0)
    def _():
        carry[...] = jnp.zeros(carry.shape, carry.dtype)

    h = h_ref[0] + carry[...]
    ple = jnp.dot(p_ref[0, 0].astype(BF16), wple_ref[...], preferred_element_type=F32)
    scales = _group_scales(stat_refs, stat_stages)

    def next_product(piece):
        if piece < len(DILATIONS):
            val = _natural_order(o_refs[piece], DILATIONS[piece], o_stages[piece]) * scales[piece]
            row = POOL_WIDTH + piece * GROUP_WIDTH
        else:
            val = _pool_mixer(tile_in_seq, u_ref, uh_ref, wpool_ref, pscale_ref)
            row = 0
        return jnp.dot(val.astype(BF16), wout_ref[row:row + GROUP_WIDTH, :], preferred_element_type=F32)

    hn = _rms(h, g2_ref[...]).astype(BF16)
    acc = h
    nxt = None
    for piece, c in enumerate(range(0, D_FF, FF_CHUNK)):
        up = jnp.dot(hn, wup_ref[:, c:c + FF_CHUNK], preferred_element_type=F32)
        act = jnp.square(jnp.maximum(up, 0.0)).astype(BF16)
        acc = acc + jnp.dot(act, wdown_ref[c:c + FF_CHUNK, :], preferred_element_type=F32)
        if piece > 0:
            prod = next_product(piece - 1)
            nxt = prod if nxt is None else nxt + prod
    h = acc

    hn = _rms(h, g3_ref[...]).astype(BF16)
    pieces = []
    for c in range(0, D, GATE_CHUNK):
        z = jnp.dot(hn, wgate_ref[:, c:c + GATE_CHUNK], preferred_element_type=F32)
        gate = 0.5 + 0.5 * jnp.tanh(0.5 * z)
        pieces.append(h[:, c:c + GATE_CHUNK] + gate * ple[:, c:c + GATE_CHUNK])
    carry[...] = nxt + next_product(len(DILATIONS))
    if final:
        ssq = sum(jnp.sum(x * x, axis=-1, keepdims=True) for x in pieces)
        inv = lax.rsqrt(ssq / D + EPS)
        pieces = [x * inv * gf_ref[:, c:c + GATE_CHUNK] for x, c in zip(pieces, range(0, D, GATE_CHUNK))]
    for x, c in zip(pieces, range(0, D, GATE_CHUNK)):
        out_ref[0, :, c:c + GATE_CHUNK] = x


def _mix(h, u, o, stats, p, layer, wpool, pscale, wout, g2, wup, wdown, g3, wgate, wple, gf, final):
    B, S, D = h.shape
    tm = ROW_TILE
    nt = S // tm
    n_tiles = B * nt
    ahead = lambda s: jnp.minimum(s, n_tiles - 1)
    behind = lambda s: jnp.maximum(s - 1, 0)
    row_b = lambda width: pl.BlockSpec((1, tm, width), lambda s: (behind(s) // nt, behind(s) % nt, 0))
    row_a = lambda width: pl.BlockSpec((1, tm, width), lambda s: (ahead(s) // nt, ahead(s) % nt, 0))
    halo = pl.BlockSpec(
        (1, POOL_HALO, POOL_WIDTH),
        lambda s: (ahead(s) // nt, jnp.maximum(ahead(s) % nt * (tm // POOL_HALO) - 1, 0), 0))
    strided = [pl.BlockSpec((1, tm // dil, dil * GROUP_WIDTH), lambda s: (ahead(s) // nt, ahead(s) % nt, 0))
               for dil in DILATIONS]
    p_spec = pl.BlockSpec((1, 1, tm, p.shape[-1]), lambda s: (layer, behind(s) // nt, behind(s) % nt, 0))
    weights = [wpool, pscale, wout, g2, wup, wdown, g3, wgate, wple, gf]
    return pl.pallas_call(
        functools.partial(_mix_kernel, final=final, tiles_per_seq=nt, n_tiles=n_tiles),
        grid=(n_tiles + 1,),
        in_specs=[row_b(D), row_a(POOL_WIDTH), halo] + strided * 2 + [p_spec]
                 + [_const_spec(w.shape) for w in weights],
        out_specs=row_b(D),
        out_shape=jax.ShapeDtypeStruct((B, S, D), F32),
        scratch_shapes=[pltpu.VMEM((tm, D), F32)]
                       + [pltpu.VMEM((GROUP_WIDTH // LANES, tm, LANES), F32) for dil in DILATIONS * 2 if dil > 1],
        compiler_params=pltpu.CompilerParams(
            dimension_semantics=("arbitrary",), vmem_limit_bytes=VMEM_LIMIT),
        name="mix_final" if final else "mix",
    )(h, u, u, *o, *stats, p, *weights)


def _rotary_tables(positions):
    inv_freq = ROPE_THETA ** (-jnp.arange(0, ROT_DIM, 2, dtype=F32) / ROT_DIM)
    ang = positions.astype(F32)[..., None] * inv_freq
    cos, sin = jnp.cos(ang), jnp.sin(ang)
    lane = np.arange(LANES)
    rotary = (lane % HEAD_DIM) < ROT_DIM
    pick = (lane[None, :] % _HALF == np.arange(_HALF)[:, None]) & rotary[None, :]
    sign = np.where(lane < HEAD_DIM, -1.0, 1.0)
    spread = lambda a, m: jnp.einsum("bsk,kl->bsl", a, jnp.asarray(m, F32), precision=lax.Precision.HIGHEST)
    cos_t = spread(cos, pick.astype(np.float32)) + jnp.asarray(~rotary, F32)
    sin_t = spread(sin, pick * sign)
    return cos_t, sin_t


def _block_diag(w):
    g, c, _ = w.shape
    eye = jnp.eye(g, dtype=w.dtype)
    return (w[:, :, None, :] * eye[:, None, :, None]).reshape(g * c, g * c)


def kernel(x, p, positions, norm1, w_in, pool_w, pool_scale, w_out, norm2, w_up, w_down, norm3, w_gate,
           w_ple, final_norm):
    depth = w_in.shape[0]
    n_grp = len(DILATIONS)
    cos_t, sin_t = _rotary_tables(positions)
    bias = _band_bias()
    vec = lambda a: a.reshape(1, -1)
    h = x
    for i in range(depth):
        u, *qkv = _inproj(h, vec(norm1[i]), _w_in_prepared(w_in[i]), cos_t, sin_t)
        o, stats = [], []
        for g, dil in enumerate(DILATIONS):
            og, sg = _attention(qkv[g], qkv[n_grp + g], qkv[2 * n_grp + g], bias, dil)
            o.append(og)
            stats.append(sg)
        h = _mix(h, u, o, stats, p, i, _block_diag(pool_w[i]).astype(BF16), vec(pool_scale[i]),
                 w_out[i].astype(BF16), vec(norm2[i]), w_up[i].astype(BF16), w_down[i].astype(BF16),
                 vec(norm3[i]), w_gate[i].astype(BF16), w_ple[i].astype(BF16), vec(final_norm),
                 final=(i == depth - 1))
    return h
```

```python
import functools

import numpy as np
import jax
import jax.numpy as jnp
from jax import lax
from jax.experimental import pallas as pl
from jax.experimental.pallas import tpu as pltpu

F32 = jnp.float32
BF16 = jnp.bfloat16

D_MODEL = 1024
HEAD_DIM = 64
POOL_WIDTH = 256
POOL_WINDOWS = (2, 4, 8, 16)
POOL_GC = POOL_WIDTH // len(POOL_WINDOWS)
ATTN_WIDTH = D_MODEL - POOL_WIDTH
DILATIONS = (1, 4, 16)
GROUP_WIDTH = ATTN_WIDTH // len(DILATIONS)
ROT_DIM = HEAD_DIM // 4
ROPE_THETA = 500000.0
BLK = 128
D_FF = 4 * D_MODEL
EPS = 1e-6
MASK_VALUE = -1e30

LANES = 128
POOL_HALO = 16
ROW_TILE = 512
ATTN_ROWS = 2048
ATTN_MIN_ROWS = 512
ATTN_UNITS = 16
STAT_SPLIT = HEAD_DIM // 2
FF_CHUNK = 1024
GATE_CHUNK = 256
VMEM_LIMIT = 56 * 1024 * 1024
WEIGHT_BLOCK = 1024
SUBLANE_STRIDE = 4


def _rms(x, g):
    return x * lax.rsqrt(jnp.mean(x * x, axis=-1, keepdims=True) + EPS) * g


def _const_spec(shape):
    zeros = (0,) * len(shape)
    return pl.BlockSpec(shape, lambda *_: zeros, pipeline_mode=pl.Buffered(1))


_HALF = ROT_DIM // 2
_QK_TILE_PERM = np.concatenate([
    np.arange(0, _HALF), np.arange(HEAD_DIM, HEAD_DIM + _HALF), np.arange(ROT_DIM, HEAD_DIM),
    np.arange(_HALF, ROT_DIM), np.arange(HEAD_DIM + _HALF, HEAD_DIM + ROT_DIM),
    np.arange(HEAD_DIM + ROT_DIM, LANES)])
_QKV_ORDER = ((0, 2), (0, 0), (1, 2), (0, 1), (2, 2), (1, 0), (1, 1), (2, 0), (2, 1))


def _residue_rows(src, mid, dil):
    tm = src.shape[0]
    if dil <= SUBLANE_STRIDE:
        for r in range(dil):
            yield r, src[pl.ds(r, tm // dil, stride=dil), :]
        return
    first, second = SUBLANE_STRIDE, dil // SUBLANE_STRIDE
    part = tm // first
    for rl in range(first):
        mid[rl * part:(rl + 1) * part, :] = src[pl.ds(rl, part, stride=first), :]
    for r in range(dil):
        rh, rl = divmod(r, first)
        yield r, mid[pl.ds(rl * part + rh, tm // dil, stride=second), :]


def _cast_kernel(w_ref, o_ref):
    o_ref[...] = w_ref[0].astype(o_ref.dtype)


def _to_bf16(w, layer):
    _, R, C = w.shape
    br, bc = min(R, WEIGHT_BLOCK), min(C, WEIGHT_BLOCK)
    return pl.pallas_call(
        _cast_kernel,
        grid=(R // br, C // bc),
        in_specs=[pl.BlockSpec((1, br, bc), lambda i, j: (layer, i, j))],
        out_specs=pl.BlockSpec((br, bc), lambda i, j: (i, j)),
        out_shape=jax.ShapeDtypeStruct((R, C), BF16),
        name="cast",
    )(w)


def _w_in_kernel(w_ref, o_ref):
    j = pl.program_id(0)
    n_grp = len(DILATIONS)
    is_qk = jnp.logical_and(j >= 1, j <= 2 * n_grp)
    scale = jnp.where(jnp.logical_and(j >= 1, j <= n_grp), HEAD_DIM ** -0.5, 1.0)
    lane = lax.broadcasted_iota(jnp.int32, (w_ref.shape[1], LANES), 1)
    swap = HEAD_DIM - _HALF
    from_high = (lane >= _HALF) & (lane < ROT_DIM)
    from_low = (lane >= HEAD_DIM) & (lane < HEAD_DIM + _HALF)
    for c in range(0, GROUP_WIDTH, LANES):
        w = w_ref[0, :, c:c + LANES]
        permuted = jnp.where(from_high, pltpu.roll(w, LANES - swap, 1), jnp.where(from_low, pltpu.roll(w, swap, 1), w))
        o_ref[:, c:c + LANES] = (jnp.where(is_qk, permuted, w) * scale).astype(o_ref.dtype)


def _w_in_prepared(w, layer):
    _, D, n_in = w.shape
    return pl.pallas_call(
        _w_in_kernel,
        grid=(n_in // GROUP_WIDTH,),
        in_specs=[pl.BlockSpec((1, D, GROUP_WIDTH), lambda j: (layer, 0, j))],
        out_specs=pl.BlockSpec((D, GROUP_WIDTH), lambda j: (0, j)),
        out_shape=jax.ShapeDtypeStruct((D, n_in), BF16),
        name="w_in_prep",
    )(w)


def _head_a_lanes(lane):
    return (lane < _HALF) | ((lane >= ROT_DIM) & (lane < HEAD_DIM + _HALF))


def _inproj_kernel(x_ref, g_ref, w_ref, cs_ref, sel_ref, u_ref, *refs):
    n_grp = len(DILATIONS)
    qkv_refs, (hnbuf, zbuf, *stages) = refs[:3 * n_grp], refs[3 * n_grp:]
    tm = x_ref.shape[1]
    s = pl.program_id(0)

    @pl.when(s == 0)
    def _():
        hnbuf[...] = jnp.zeros(hnbuf.shape, hnbuf.dtype)
        zbuf[...] = jnp.zeros(zbuf.shape, zbuf.dtype)

    table = jnp.dot(cs_ref[0], sel_ref[...], preferred_element_type=F32)
    lane = lax.broadcasted_iota(jnp.int32, (1, LANES), 1)
    cos_t = table[:, :LANES] + ((lane % HEAD_DIM) >= ROT_DIM).astype(F32)
    sin_t = table[:, LANES:]
    hn = hnbuf[...]

    def column(kind, g):
        return POOL_WIDTH + kind * ATTN_WIDTH + g * GROUP_WIDTH

    for b, (kind, g) in enumerate(_QKV_ORDER):
        out_ref = qkv_refs[kind * n_grp + g]
        if b == 0:
            z = zbuf[...]
        else:
            z = jnp.dot(hn, w_ref[:, column(kind, g):column(kind, g) + GROUP_WIDTH], preferred_element_type=F32)
        halves = [z[:, c:c + LANES] for c in range(0, GROUP_WIDTH, LANES)]
        if kind < 2:
            halves = [z_h * cos_t + pltpu.roll(z_h, LANES // 2, 1) * sin_t for z_h in halves]
        dil = DILATIONS[g]
        if dil == 1:
            out_ref[0] = jnp.concatenate(halves, axis=-1).astype(out_ref.dtype)
            continue
        src, mid = stages[2 * (b % 2)], stages[2 * (b % 2) + 1]
        for t, z_h in enumerate(halves):
            src[t] = z_h
            for r, rows in _residue_rows(src.at[t], mid.at[t], dil):
                c = r * GROUP_WIDTH + t * LANES
                out_ref[0, :, c:c + LANES] = rows.astype(out_ref.dtype)
    u_ref[0] = jnp.dot(hn, w_ref[:, :POOL_WIDTH], preferred_element_type=F32)

    hn_next = _rms(x_ref[0], g_ref[...]).astype(BF16)
    first = column(*_QKV_ORDER[0])
    zbuf[...] = jnp.dot(hn_next, w_ref[:, first:first + GROUP_WIDTH], preferred_element_type=F32)
    hnbuf[...] = hn_next


def _inproj(x, g, w_in, cs, sel):
    B, S, D = x.shape
    n_in = w_in.shape[1]
    tm = ROW_TILE
    nt = S // tm
    n_tiles = B * nt
    ahead = lambda s: jnp.minimum(s, n_tiles - 1)
    behind = lambda s: jnp.maximum(s - 1, 0)
    row = lambda width, tile: pl.BlockSpec((1, tm, width), lambda s: (tile(s) // nt, tile(s) % nt, 0))
    strided = lambda dil: pl.BlockSpec((1, tm // dil, dil * GROUP_WIDTH),
                                       lambda s: (behind(s) // nt, behind(s) % nt, 0))
    out_shape = [jax.ShapeDtypeStruct((B, S, POOL_WIDTH), F32)]
    out_shape += [jax.ShapeDtypeStruct((B, S // dil, dil * GROUP_WIDTH), BF16) for dil in DILATIONS] * 3
    return pl.pallas_call(
        _inproj_kernel,
        grid=(n_tiles + 1,),
        in_specs=[row(D, ahead), _const_spec((1, D)), _const_spec((D, n_in)), row(cs.shape[-1], behind),
                  _const_spec(sel.shape)],
        out_specs=[row(POOL_WIDTH, behind)] + [strided(dil) for dil in DILATIONS] * 3,
        out_shape=out_shape,
        scratch_shapes=[pltpu.VMEM((tm, D), BF16), pltpu.VMEM((tm, GROUP_WIDTH), F32)]
                       + [pltpu.VMEM((GROUP_WIDTH // LANES, tm, LANES), F32)] * 4,
        compiler_params=pltpu.CompilerParams(
            dimension_semantics=("arbitrary",), vmem_limit_bytes=VMEM_LIMIT),
        name="inproj",
    )(x, g, w_in, cs, sel)


def _attn_kernel(q_ref, k_ref, kh_ref, v_ref, vh_ref, bias_ref, o_ref, stat_ref, kbuf, vbuf):
    rows, cols = q_ref.shape[1], q_ref.shape[2]
    n_blk = rows // BLK
    first_chunk = pl.program_id(1) == 0

    kbuf[0:BLK] = kh_ref[0]
    kbuf[BLK:] = k_ref[0]
    vbuf[0:BLK] = vh_ref[0]
    vbuf[BLK:] = v_ref[0]

    lane = lax.broadcasted_iota(jnp.int32, (BLK, LANES), 1)
    head0 = lane < HEAD_DIM
    head0_qk = _head_a_lanes(lane)
    stat_is_max = (lane % HEAD_DIM) < STAT_SPLIT
    ones = jnp.ones((2 * BLK, LANES), BF16)

    def unit(i, c):
        r0 = pl.multiple_of(i * BLK, BLK)
        q = q_ref[0, pl.ds(r0, BLK), c:c + LANES]
        kwin = kbuf[pl.ds(r0, 2 * BLK), c:c + LANES]
        vwin = vbuf[pl.ds(r0, 2 * BLK), c:c + LANES]
        zero = jnp.zeros_like(q)
        q2 = jnp.concatenate([jnp.where(head0_qk, q, zero), jnp.where(head0_qk, zero, q)], axis=0)
        s = lax.dot_general(q2, kwin, (((1,), (1,)), ((), ())), preferred_element_type=F32)
        no_history = jnp.logical_and(first_chunk, i == 0).astype(jnp.int32)
        s = s + bias_ref[no_history]
        m = jnp.max(s, axis=-1, keepdims=True)
        e = jnp.exp(s - m).astype(BF16)
        pv = jnp.dot(e, jnp.concatenate([vwin, ones], axis=1), preferred_element_type=F32)
        o_ref[0, pl.ds(r0, BLK), c:c + LANES] = jnp.where(
            head0, pv[:BLK, :LANES], pv[BLK:, :LANES]).astype(o_ref.dtype)
        m_both = jnp.where(head0, jnp.broadcast_to(m[:BLK], (BLK, LANES)), jnp.broadcast_to(m[BLK:], (BLK, LANES)))
        l_both = jnp.where(head0, pv[:BLK, LANES:], pv[BLK:, LANES:])
        stat_ref[0, pl.ds(r0, BLK), c:c + LANES] = jnp.where(stat_is_max, m_both, l_both)

    tiles = cols // LANES
    tiles_per_body = min(tiles, ATTN_UNITS)
    blocks_per_body = ATTN_UNITS // tiles_per_body
    for c0 in range(0, tiles, tiles_per_body):
        def body(ib, carry, c0=c0):
            for bi in range(blocks_per_body):
                for ci in range(tiles_per_body):
                    unit(ib * blocks_per_body + bi, (c0 + ci) * LANES)
            return carry
        lax.fori_loop(0, n_blk // blocks_per_body, body, 0)


def _attention(q, k, v, bias, dil):
    B, L, width = q.shape
    W = width // dil
    rows = min(L, max(ATTN_ROWS // dil, ATTN_MIN_ROWS))
    cols = (ATTN_ROWS // rows) * W
    hist_blocks = rows // BLK
    main = pl.BlockSpec((1, rows, cols), lambda b, n, r: (b, n, r))
    hist = pl.BlockSpec((1, BLK, cols), lambda b, n, r: (b, jnp.maximum(n * hist_blocks - 1, 0), r))
    return pl.pallas_call(
        _attn_kernel,
        grid=(B, L // rows, dil * W // cols),
        in_specs=[main, main, hist, main, hist, _const_spec(bias.shape)],
        out_specs=[main, main],
        out_shape=[jax.ShapeDtypeStruct((B, L, dil * W), BF16), jax.ShapeDtypeStruct((B, L, dil * W), F32)],
        scratch_shapes=[pltpu.VMEM((BLK + rows, cols), BF16), pltpu.VMEM((BLK + rows, cols), BF16)],
        compiler_params=pltpu.CompilerParams(
            dimension_semantics=("parallel", "arbitrary", "parallel"), vmem_limit_bytes=VMEM_LIMIT),
        name=f"attn_d{dil}",
    )(q, k, k, v, v, bias)


def _band_bias():
    a = jnp.arange(2 * BLK)[:, None] % BLK
    c = jnp.arange(2 * BLK)[None, :]
    band = (c >= a) & (c <= a + BLK)
    both = jnp.stack([band, band & (c >= BLK)])
    return jnp.where(both, 0.0, MASK_VALUE).astype(F32)


def _stage_buffers(dil):
    return 0 if dil == 1 else 1 if dil <= SUBLANE_STRIDE else 2


def _natural_order(ref, dil, stage):
    if dil == 1:
        return ref[0].astype(F32)
    n = ref.shape[1]
    tm = n * dil
    tiles = GROUP_WIDTH // LANES
    dst, mid = stage
    first = min(dil, SUBLANE_STRIDE)
    second, part = dil // first, tm // first
    for t in range(tiles):
        for r in range(dil):
            rh, rl = divmod(r, first)
            c = r * GROUP_WIDTH + t * LANES
            rows = ref[0, :, c:c + LANES].astype(F32)
            if second == 1:
                dst[t, pl.ds(r, n, stride=dil), :] = rows
            else:
                mid[t, pl.ds(rl * part + rh, n, stride=second), :] = rows
        if second > 1:
            for rl in range(first):
                dst[t, pl.ds(rl, part, stride=first), :] = mid[t, rl * part:(rl + 1) * part, :]
    return jnp.concatenate([dst[t] for t in range(tiles)], axis=-1)


def _pool_mixer(tile_in_seq, u_ref, uh_ref, wpool_ref, pscale_ref):
    tm = u_ref.shape[1]
    hist = jnp.where(tile_in_seq == 0, 0.0, uh_ref[0])
    ub = jnp.concatenate([hist, u_ref[0]], axis=0)
    s2 = ub + pltpu.roll(ub, 1, 0)
    s4 = s2 + pltpu.roll(s2, 2, 0)
    s8 = s4 + pltpu.roll(s4, 4, 0)
    s16 = s8 + pltpu.roll(s8, 8, 0)
    t = lax.broadcasted_iota(jnp.int32, (tm, LANES), 0) + tile_in_seq * tm + 1
    lane = lax.broadcasted_iota(jnp.int32, (tm, LANES), 1)
    low = lane < POOL_GC
    cnt = lambda w: jnp.minimum(t, w).astype(F32)
    tile = lambda s, c: s[POOL_HALO:, c:c + LANES]
    y_lo = jnp.where(low, tile(s2, 0) / cnt(2), tile(s4, 0) / cnt(4)) - tile(ub, 0)
    y_hi = jnp.where(low, tile(s8, LANES) / cnt(8), tile(s16, LANES) / cnt(16)) - tile(ub, LANES)
    y = jnp.concatenate([y_lo, y_hi], axis=-1).astype(BF16)
    return jnp.dot(y, wpool_ref[...], preferred_element_type=F32) * pscale_ref[...]


def _group_scales(stat_refs, stages):
    stat = [_natural_order(r, d, stage) for r, d, stage in zip(stat_refs, DILATIONS, stages)]
    lane = lax.broadcasted_iota(jnp.int32, stat[0].shape, 1)
    is_max = (lane % HEAD_DIM) < STAT_SPLIT
    m = [jnp.where(is_max, s, pltpu.roll(s, STAT_SPLIT, 1)) for s in stat]
    l = [jnp.where(is_max, pltpu.roll(s, GROUP_WIDTH - STAT_SPLIT, 1), s) for s in stat]
    mx = jnp.maximum(jnp.maximum(m[0], m[1]), m[2])
    e = [jnp.exp(m_g - mx) for m_g in m]
    inv = 1.0 / (l[0] * e[0] + l[1] * e[1] + l[2] * e[2])
    return [e_g * inv for e_g in e]


def _mix_kernel(h_ref, u_ref, uh_ref, o1_ref, o4_ref, o16_ref, l1_ref, l4_ref, l16_ref, p_ref,
                wpool_ref, pscale_ref, wout_ref, g2_ref, wup_ref, wdown_ref, g3_ref, wgate_ref, wple_ref,
                gf_ref, out_ref, carry, *stages, final, tiles_per_seq, n_tiles):
    s = pl.program_id(0)
    D = h_ref.shape[2]
    tile_in_seq = jnp.minimum(s, n_tiles - 1) % tiles_per_seq
    o_refs, stat_refs = (o1_ref, o4_ref, o16_ref), (l1_ref, l4_ref, l16_ref)
    stage = iter(stages)
    take = lambda d: [next(stage) for _ in range(_stage_buffers(d))] + [None] * (2 - _stage_buffers(d))
    o_stages = [take(d) for d in DILATIONS]
    stat_stages = [take(d) for d in DILATIONS]

    @pl.when(s == 0)
    def _():
        carry[...] = jnp.zeros(carry.shape, carry.dtype)

    h = h_ref[0] + carry[...]
    ple = jnp.dot(p_ref[0, 0].astype(BF16), wple_ref[...], preferred_element_type=F32)
    scales = _group_scales(stat_refs, stat_stages)

    def next_product(piece):
        if piece < len(DILATIONS):
            val = _natural_order(o_refs[piece], DILATIONS[piece], o_stages[piece]) * scales[piece]
            row = POOL_WIDTH + piece * GROUP_WIDTH
        else:
            val = _pool_mixer(tile_in_seq, u_ref, uh_ref, wpool_ref, pscale_ref)
            row = 0
        return jnp.dot(val.astype(BF16), wout_ref[row:row + GROUP_WIDTH, :], preferred_element_type=F32)

    hn = _rms(h, g2_ref[...]).astype(BF16)
    acc = h
    nxt = None
    for piece, c in enumerate(range(0, D_FF, FF_CHUNK)):
        up = jnp.dot(hn, wup_ref[:, c:c + FF_CHUNK], preferred_element_type=F32)
        act = jnp.square(jnp.maximum(up, 0.0)).astype(BF16)
        acc = acc + jnp.dot(act, wdown_ref[c:c + FF_CHUNK, :], preferred_element_type=F32)
        if piece > 0:
            prod = next_product(piece - 1)
            nxt = prod if nxt is None else nxt + prod
    h = acc

    hn = _rms(h, g3_ref[...]).astype(BF16)
    pieces = []
    for c in range(0, D, GATE_CHUNK):
        z = jnp.dot(hn, wgate_ref[:, c:c + GATE_CHUNK], preferred_element_type=F32)
        gate = 0.5 + 0.5 * jnp.tanh(0.5 * z)
        pieces.append(h[:, c:c + GATE_CHUNK] + gate * ple[:, c:c + GATE_CHUNK])
    carry[...] = nxt + next_product(len(DILATIONS))
    if final:
        ssq = sum(jnp.sum(x * x, axis=-1, keepdims=True) for x in pieces)
        inv = lax.rsqrt(ssq / D + EPS)
        pieces = [x * inv * gf_ref[:, c:c + GATE_CHUNK] for x, c in zip(pieces, range(0, D, GATE_CHUNK))]
    for x, c in zip(pieces, range(0, D, GATE_CHUNK)):
        out_ref[0, :, c:c + GATE_CHUNK] = x


def _mix(h, u, o, stats, p, layer, wpool, pscale, wout, g2, wup, wdown, g3, wgate, wple, gf, final):
    B, S, D = h.shape
    tm = ROW_TILE
    nt = S // tm
    n_tiles = B * nt
    ahead = lambda s: jnp.minimum(s, n_tiles - 1)
    behind = lambda s: jnp.maximum(s - 1, 0)
    row_b = lambda width: pl.BlockSpec((1, tm, width), lambda s: (behind(s) // nt, behind(s) % nt, 0))
    row_a = lambda width: pl.BlockSpec((1, tm, width), lambda s: (ahead(s) // nt, ahead(s) % nt, 0))
    halo = pl.BlockSpec(
        (1, POOL_HALO, POOL_WIDTH),
        lambda s: (ahead(s) // nt, jnp.maximum(ahead(s) % nt * (tm // POOL_HALO) - 1, 0), 0))
    strided = [pl.BlockSpec((1, tm // dil, dil * GROUP_WIDTH), lambda s: (ahead(s) // nt, ahead(s) % nt, 0))
               for dil in DILATIONS]
    p_spec = pl.BlockSpec((1, 1, tm, p.shape[-1]), lambda s: (layer, behind(s) // nt, behind(s) % nt, 0))
    weights = [wpool, pscale, wout, g2, wup, wdown, g3, wgate, wple, gf]
    return pl.pallas_call(
        functools.partial(_mix_kernel, final=final, tiles_per_seq=nt, n_tiles=n_tiles),
        grid=(n_tiles + 1,),
        in_specs=[row_b(D), row_a(POOL_WIDTH), halo] + strided * 2 + [p_spec]
                 + [_const_spec(w.shape) for w in weights],
        out_specs=row_b(D),
        out_shape=jax.ShapeDtypeStruct((B, S, D), F32),
        scratch_shapes=[pltpu.VMEM((tm, D), F32)]
                       + [pltpu.VMEM((GROUP_WIDTH // LANES, tm, LANES), F32)
                          for dil in DILATIONS * 2 for _ in range(_stage_buffers(dil))],
        compiler_params=pltpu.CompilerParams(
            dimension_semantics=("arbitrary",), vmem_limit_bytes=VMEM_LIMIT),
        name="mix_final" if final else "mix",
    )(h, u, u, *o, *stats, p, *weights)


def _rotary_parts(positions):
    inv_freq = ROPE_THETA ** (-jnp.arange(0, ROT_DIM, 2, dtype=F32) / ROT_DIM)
    ang = positions.astype(F32)[..., None] * inv_freq
    cs = jnp.concatenate([jnp.cos(ang), jnp.sin(ang)], axis=-1)
    hi = cs.astype(BF16)
    rest = cs - hi.astype(F32)
    mid = rest.astype(BF16)
    lo = (rest - mid.astype(F32)).astype(BF16)
    lane = np.arange(LANES)
    rotary = (lane % HEAD_DIM) < ROT_DIM
    pick = ((lane[None, :] % _HALF == np.arange(_HALF)[:, None]) & rotary[None, :]).astype(np.float32)
    sign = np.where(lane < HEAD_DIM, -1.0, 1.0).astype(np.float32)
    zero = np.zeros_like(pick)
    one_part = np.block([[pick, zero], [zero, pick * sign]])
    sel = np.concatenate([one_part] * 3, axis=0)
    return jnp.concatenate([hi, mid, lo], axis=-1), jnp.asarray(sel, BF16)


def _block_diag(w):
    g, c, _ = w.shape
    eye = jnp.eye(g, dtype=w.dtype)
    return (w[:, :, None, :] * eye[:, None, :, None]).reshape(g * c, g * c)


def kernel(x, p, positions, norm1, w_in, pool_w, pool_scale, w_out, norm2, w_up, w_down, norm3, w_gate,
           w_ple, final_norm):
    depth = w_in.shape[0]
    n_grp = len(DILATIONS)
    cs, sel = _rotary_parts(positions)
    bias = _band_bias()
    vec = lambda a: a.reshape(1, -1)
    h = x
    for i in range(depth):
        u, *qkv = _inproj(h, vec(norm1[i]), _w_in_prepared(w_in, i), cs, sel)
        o, stats = [], []
        for g, dil in enumerate(DILATIONS):
            og, sg = _attention(qkv[g], qkv[n_grp + g], qkv[2 * n_grp + g], bias, dil)
            o.append(og)
            stats.append(sg)
        h = _mix(h, u, o, stats, p, i, _block_diag(pool_w[i]).astype(BF16), vec(pool_scale[i]),
                 _to_bf16(w_out, i), vec(norm2[i]), _to_bf16(w_up, i), _to_bf16(w_down, i),
                 vec(norm3[i]), _to_bf16(w_gate, i), _to_bf16(w_ple, i), vec(final_norm),
                 final=(i == depth - 1))
    return h
```

```python
import functools

import numpy as np
import jax
import jax.numpy as jnp
from jax import lax
from jax.experimental import pallas as pl
from jax.experimental.pallas import tpu as pltpu

F32 = jnp.float32
BF16 = jnp.bfloat16

D_MODEL = 1024
HEAD_DIM = 64
POOL_WIDTH = 256
POOL_WINDOWS = (2, 4, 8, 16)
POOL_GC = POOL_WIDTH // len(POOL_WINDOWS)
ATTN_WIDTH = D_MODEL - POOL_WIDTH
DILATIONS = (1, 4, 16)
GROUP_WIDTH = ATTN_WIDTH // len(DILATIONS)
ROT_DIM = HEAD_DIM // 4
ROPE_THETA = 500000.0
BLK = 128
D_FF = 4 * D_MODEL
EPS = 1e-6
MASK_VALUE = -1e30

LANES = 128
POOL_HALO = 16
ROW_TILE = 512
ATTN_ROWS = 4096
ATTN_MIN_ROWS = 512
ATTN_UNITS = 16
STAT_SPLIT = HEAD_DIM // 2
FF_CHUNK = 1024
GATE_CHUNK = 256
VMEM_LIMIT = 56 * 1024 * 1024
WEIGHT_BLOCK = 1024
SUBLANE_STRIDE = 4


def _rms(x, g):
    return x * lax.rsqrt(jnp.mean(x * x, axis=-1, keepdims=True) + EPS) * g


def _const_spec(shape):
    zeros = (0,) * len(shape)
    return pl.BlockSpec(shape, lambda *_: zeros, pipeline_mode=pl.Buffered(1))


_HALF = ROT_DIM // 2
_QK_TILE_PERM = np.concatenate([
    np.arange(0, _HALF), np.arange(HEAD_DIM, HEAD_DIM + _HALF), np.arange(ROT_DIM, HEAD_DIM),
    np.arange(_HALF, ROT_DIM), np.arange(HEAD_DIM + _HALF, HEAD_DIM + ROT_DIM),
    np.arange(HEAD_DIM + ROT_DIM, LANES)])
_QKV_ORDER = ((0, 2), (0, 0), (1, 2), (0, 1), (2, 2), (1, 0), (1, 1), (2, 0), (2, 1))


def _residue_rows(src, mid, dil):
    tm = src.shape[0]
    if dil <= SUBLANE_STRIDE:
        for r in range(dil):
            yield r, src[pl.ds(r, tm // dil, stride=dil), :]
        return
    first, second = SUBLANE_STRIDE, dil // SUBLANE_STRIDE
    part = tm // first
    for rl in range(first):
        mid[rl * part:(rl + 1) * part, :] = src[pl.ds(rl, part, stride=first), :]
    for r in range(dil):
        rh, rl = divmod(r, first)
        yield r, mid[pl.ds(rl * part + rh, tm // dil, stride=second), :]


def _cast_kernel(w_ref, o_ref):
    o_ref[...] = w_ref[0].astype(o_ref.dtype)


def _to_bf16(w, layer):
    _, R, C = w.shape
    br, bc = min(R, WEIGHT_BLOCK), min(C, WEIGHT_BLOCK)
    return pl.pallas_call(
        _cast_kernel,
        grid=(R // br, C // bc),
        in_specs=[pl.BlockSpec((1, br, bc), lambda i, j: (layer, i, j))],
        out_specs=pl.BlockSpec((br, bc), lambda i, j: (i, j)),
        out_shape=jax.ShapeDtypeStruct((R, C), BF16),
        name="cast",
    )(w)


def _w_in_kernel(w_ref, o_ref):
    j = pl.program_id(0)
    n_grp = len(DILATIONS)
    is_qk = jnp.logical_and(j >= 1, j <= 2 * n_grp)
    scale = jnp.where(jnp.logical_and(j >= 1, j <= n_grp), HEAD_DIM ** -0.5, 1.0)
    lane = lax.broadcasted_iota(jnp.int32, (w_ref.shape[1], LANES), 1)
    swap = HEAD_DIM - _HALF
    from_high = (lane >= _HALF) & (lane < ROT_DIM)
    from_low = (lane >= HEAD_DIM) & (lane < HEAD_DIM + _HALF)
    for c in range(0, GROUP_WIDTH, LANES):
        w = w_ref[0, :, c:c + LANES]
        permuted = jnp.where(from_high, pltpu.roll(w, LANES - swap, 1), jnp.where(from_low, pltpu.roll(w, swap, 1), w))
        o_ref[:, c:c + LANES] = (jnp.where(is_qk, permuted, w) * scale).astype(o_ref.dtype)


def _w_in_prepared(w, layer):
    _, D, n_in = w.shape
    return pl.pallas_call(
        _w_in_kernel,
        grid=(n_in // GROUP_WIDTH,),
        in_specs=[pl.BlockSpec((1, D, GROUP_WIDTH), lambda j: (layer, 0, j))],
        out_specs=pl.BlockSpec((D, GROUP_WIDTH), lambda j: (0, j)),
        out_shape=jax.ShapeDtypeStruct((D, n_in), BF16),
        name="w_in_prep",
    )(w)


def _head_a_lanes(lane):
    return (lane < _HALF) | ((lane >= ROT_DIM) & (lane < HEAD_DIM + _HALF))


def _inproj_kernel(x_ref, g_ref, w_ref, cs_ref, sel_ref, u_ref, *refs):
    n_grp = len(DILATIONS)
    qkv_refs, (hnbuf, zbuf, *stages) = refs[:3 * n_grp], refs[3 * n_grp:]
    tm = x_ref.shape[1]
    s = pl.program_id(0)

    def column(kind, g):
        return POOL_WIDTH + kind * ATTN_WIDTH + g * GROUP_WIDTH

    def prepare_next():
        hn_next = _rms(x_ref[0], g_ref[...]).astype(BF16)
        first = column(*_QKV_ORDER[0])
        zbuf[...] = jnp.dot(hn_next, w_ref[:, first:first + GROUP_WIDTH], preferred_element_type=F32)
        hnbuf[...] = hn_next

    @pl.when(s == 0)
    def _():
        prepare_next()

    @pl.when(s > 0)
    def _():
        table = jnp.dot(cs_ref[0], sel_ref[...], preferred_element_type=F32)
        lane = lax.broadcasted_iota(jnp.int32, (1, LANES), 1)
        cos_t = table[:, :LANES] + ((lane % HEAD_DIM) >= ROT_DIM).astype(F32)
        sin_t = table[:, LANES:]
        hn = hnbuf[...]

        for b, (kind, g) in enumerate(_QKV_ORDER):
            out_ref = qkv_refs[kind * n_grp + g]
            if b == 0:
                z = zbuf[...]
            else:
                col = column(kind, g)
                z = jnp.dot(hn, w_ref[:, col:col + GROUP_WIDTH], preferred_element_type=F32)
            halves = [z[:, c:c + LANES] for c in range(0, GROUP_WIDTH, LANES)]
            if kind < 2:
                halves = [z_h * cos_t + pltpu.roll(z_h, LANES // 2, 1) * sin_t for z_h in halves]
            dil = DILATIONS[g]
            if dil == 1:
                out_ref[0] = jnp.concatenate(halves, axis=-1).astype(out_ref.dtype)
                continue
            src, mid = stages[2 * (b % 2)], stages[2 * (b % 2) + 1]
            for t, z_h in enumerate(halves):
                src[t] = z_h
                for r, rows in _residue_rows(src.at[t], mid.at[t], dil):
                    c = r * GROUP_WIDTH + t * LANES
                    out_ref[0, :, c:c + LANES] = rows.astype(out_ref.dtype)
        u_ref[0] = jnp.dot(hn, w_ref[:, :POOL_WIDTH], preferred_element_type=F32)
        prepare_next()


def _inproj(x, g, w_in, cs, sel):
    B, S, D = x.shape
    n_in = w_in.shape[1]
    tm = ROW_TILE
    nt = S // tm
    n_tiles = B * nt
    ahead = lambda s: jnp.minimum(s, n_tiles - 1)
    behind = lambda s: jnp.maximum(s - 1, 0)
    row = lambda width, tile: pl.BlockSpec((1, tm, width), lambda s: (tile(s) // nt, tile(s) % nt, 0))
    strided = lambda dil: pl.BlockSpec((1, tm // dil, dil * GROUP_WIDTH),
                                       lambda s: (behind(s) // nt, behind(s) % nt, 0))
    out_shape = [jax.ShapeDtypeStruct((B, S, POOL_WIDTH), F32)]
    out_shape += [jax.ShapeDtypeStruct((B, S // dil, dil * GROUP_WIDTH), BF16) for dil in DILATIONS] * 3
    return pl.pallas_call(
        _inproj_kernel,
        grid=(n_tiles + 1,),
        in_specs=[row(D, ahead), _const_spec((1, D)), _const_spec((D, n_in)), row(cs.shape[-1], behind),
                  _const_spec(sel.shape)],
        out_specs=[row(POOL_WIDTH, behind)] + [strided(dil) for dil in DILATIONS] * 3,
        out_shape=out_shape,
        scratch_shapes=[pltpu.VMEM((tm, D), BF16), pltpu.VMEM((tm, GROUP_WIDTH), F32)]
                       + [pltpu.VMEM((GROUP_WIDTH // LANES, tm, LANES), F32)] * 4,
        compiler_params=pltpu.CompilerParams(
            dimension_semantics=("arbitrary",), vmem_limit_bytes=VMEM_LIMIT),
        name="inproj",
    )(x, g, w_in, cs, sel)


def _attn_kernel(q_ref, k_ref, kh_ref, v_ref, vh_ref, bias_ref, o_ref, stat_ref, kbuf, vbuf):
    rows, cols = q_ref.shape[1], q_ref.shape[2]
    n_blk = rows // BLK
    first_chunk = pl.program_id(1) == 0

    kbuf[0:BLK] = kh_ref[0]
    kbuf[BLK:] = k_ref[0]
    vbuf[0:BLK] = vh_ref[0]
    vbuf[BLK:] = v_ref[0]

    lane = lax.broadcasted_iota(jnp.int32, (BLK, LANES), 1)
    head0 = lane < HEAD_DIM
    head0_qk = _head_a_lanes(lane)
    stat_is_max = (lane % HEAD_DIM) < STAT_SPLIT
    ones = jnp.ones((2 * BLK, LANES), BF16)

    def unit(i, c):
        r0 = pl.multiple_of(i * BLK, BLK)
        q = q_ref[0, pl.ds(r0, BLK), c:c + LANES]
        kwin = kbuf[pl.ds(r0, 2 * BLK), c:c + LANES]
        vwin = vbuf[pl.ds(r0, 2 * BLK), c:c + LANES]
        zero = jnp.zeros_like(q)
        q2 = jnp.concatenate([jnp.where(head0_qk, q, zero), jnp.where(head0_qk, zero, q)], axis=0)
        s = lax.dot_general(q2, kwin, (((1,), (1,)), ((), ())), preferred_element_type=F32)
        no_history = jnp.logical_and(first_chunk, i == 0).astype(jnp.int32)
        s = s + bias_ref[no_history]
        m = jnp.max(s, axis=-1, keepdims=True)
        e = jnp.exp(s - m).astype(BF16)
        pv = jnp.dot(e, jnp.concatenate([vwin, ones], axis=1), preferred_element_type=F32)
        o_ref[0, pl.ds(r0, BLK), c:c + LANES] = jnp.where(
            head0, pv[:BLK, :LANES], pv[BLK:, :LANES]).astype(o_ref.dtype)
        m_both = jnp.where(head0, jnp.broadcast_to(m[:BLK], (BLK, LANES)), jnp.broadcast_to(m[BLK:], (BLK, LANES)))
        l_both = jnp.where(head0, pv[:BLK, LANES:], pv[BLK:, LANES:])
        stat_ref[0, pl.ds(r0, BLK), c:c + LANES] = jnp.where(stat_is_max, m_both, l_both)

    tiles = cols // LANES
    tiles_per_body = min(tiles, ATTN_UNITS)
    blocks_per_body = ATTN_UNITS // tiles_per_body
    for c0 in range(0, tiles, tiles_per_body):
        def body(ib, carry, c0=c0):
            for bi in range(blocks_per_body):
                for ci in range(tiles_per_body):
                    unit(ib * blocks_per_body + bi, (c0 + ci) * LANES)
            return carry
        lax.fori_loop(0, n_blk // blocks_per_body, body, 0)


def _attention(q, k, v, bias, dil):
    B, L, width = q.shape
    W = width // dil
    rows = min(L, max(ATTN_ROWS // dil, ATTN_MIN_ROWS))
    cols = (ATTN_ROWS // rows) * W
    hist_blocks = rows // BLK
    main = pl.BlockSpec((1, rows, cols), lambda b, n, r: (b, n, r))
    hist = pl.BlockSpec((1, BLK, cols), lambda b, n, r: (b, jnp.maximum(n * hist_blocks - 1, 0), r))
    return pl.pallas_call(
        _attn_kernel,
        grid=(B, L // rows, dil * W // cols),
        in_specs=[main, main, hist, main, hist, _const_spec(bias.shape)],
        out_specs=[main, main],
        out_shape=[jax.ShapeDtypeStruct((B, L, dil * W), BF16), jax.ShapeDtypeStruct((B, L, dil * W), F32)],
        scratch_shapes=[pltpu.VMEM((BLK + rows, cols), BF16), pltpu.VMEM((BLK + rows, cols), BF16)],
        compiler_params=pltpu.CompilerParams(
            dimension_semantics=("parallel", "arbitrary", "parallel"), vmem_limit_bytes=VMEM_LIMIT),
        name=f"attn_d{dil}",
    )(q, k, k, v, v, bias)


def _band_bias():
    a = jnp.arange(2 * BLK)[:, None] % BLK
    c = jnp.arange(2 * BLK)[None, :]
    band = (c >= a) & (c <= a + BLK)
    both = jnp.stack([band, band & (c >= BLK)])
    return jnp.where(both, 0.0, MASK_VALUE).astype(F32)


def _stage_buffers(dil):
    return 0 if dil == 1 else 1 if dil <= SUBLANE_STRIDE else 2


def _natural_order(ref, dil, stage):
    if dil == 1:
        return ref[0].astype(F32)
    n = ref.shape[1]
    tm = n * dil
    tiles = GROUP_WIDTH // LANES
    dst, mid = stage
    first = min(dil, SUBLANE_STRIDE)
    second, part = dil // first, tm // first
    for t in range(tiles):
        for r in range(dil):
            rh, rl = divmod(r, first)
            c = r * GROUP_WIDTH + t * LANES
            rows = ref[0, :, c:c + LANES].astype(F32)
            if second == 1:
                dst[t, pl.ds(r, n, stride=dil), :] = rows
            else:
                mid[t, pl.ds(rl * part + rh, n, stride=second), :] = rows
        if second > 1:
            for rl in range(first):
                dst[t, pl.ds(rl, part, stride=first), :] = mid[t, rl * part:(rl + 1) * part, :]
    return jnp.concatenate([dst[t] for t in range(tiles)], axis=-1)


def _pool_mixer(tile_in_seq, u_ref, uh_ref, wpool_ref, pscale_ref):
    tm = u_ref.shape[1]
    hist = jnp.where(tile_in_seq == 0, 0.0, uh_ref[0])
    ub = jnp.concatenate([hist, u_ref[0]], axis=0)
    s2 = ub + pltpu.roll(ub, 1, 0)
    s4 = s2 + pltpu.roll(s2, 2, 0)
    s8 = s4 + pltpu.roll(s4, 4, 0)
    s16 = s8 + pltpu.roll(s8, 8, 0)
    t = lax.broadcasted_iota(jnp.int32, (tm, LANES), 0) + tile_in_seq * tm + 1
    lane = lax.broadcasted_iota(jnp.int32, (tm, LANES), 1)
    low = lane < POOL_GC
    cnt = lambda w: jnp.minimum(t, w).astype(F32)
    tile = lambda s, c: s[POOL_HALO:, c:c + LANES]
    y_lo = jnp.where(low, tile(s2, 0) / cnt(2), tile(s4, 0) / cnt(4)) - tile(ub, 0)
    y_hi = jnp.where(low, tile(s8, LANES) / cnt(8), tile(s16, LANES) / cnt(16)) - tile(ub, LANES)
    y = jnp.concatenate([y_lo, y_hi], axis=-1).astype(BF16)
    return jnp.dot(y, wpool_ref[...], preferred_element_type=F32) * pscale_ref[...]


def _group_scales(stat_refs, stages):
    stat = [_natural_order(r, d, stage) for r, d, stage in zip(stat_refs, DILATIONS, stages)]
    lane = lax.broadcasted_iota(jnp.int32, stat[0].shape, 1)
    is_max = (lane % HEAD_DIM) < STAT_SPLIT
    m = [jnp.where(is_max, s, pltpu.roll(s, STAT_SPLIT, 1)) for s in stat]
    l = [jnp.where(is_max, pltpu.roll(s, GROUP_WIDTH - STAT_SPLIT, 1), s) for s in stat]
    mx = jnp.maximum(jnp.maximum(m[0], m[1]), m[2])
    e = [jnp.exp(m_g - mx) for m_g in m]
    inv = 1.0 / (l[0] * e[0] + l[1] * e[1] + l[2] * e[2])
    return [e_g * inv for e_g in e]


def _mix_kernel(h_ref, u_ref, uh_ref, o1_ref, o4_ref, o16_ref, l1_ref, l4_ref, l16_ref, p_ref,
                wpool_ref, pscale_ref, wout_ref, g2_ref, wup_ref, wdown_ref, g3_ref, wgate_ref, wple_ref,
                gf_ref, out_ref, carry, *stages, final, tiles_per_seq, n_tiles):
    s = pl.program_id(0)
    D = h_ref.shape[2]
    tile_in_seq = jnp.minimum(s, n_tiles - 1) % tiles_per_seq
    o_refs, stat_refs = (o1_ref, o4_ref, o16_ref), (l1_ref, l4_ref, l16_ref)
    stage = iter(stages)
    take = lambda d: [next(stage) for _ in range(_stage_buffers(d))] + [None] * (2 - _stage_buffers(d))
    o_stages = [take(d) for d in DILATIONS]
    stat_stages = [take(d) for d in DILATIONS]

    def next_products():
        def product(val, row):
            return jnp.dot(val.astype(BF16), wout_ref[row:row + GROUP_WIDTH, :], preferred_element_type=F32)
        pool = product(_pool_mixer(tile_in_seq, u_ref, uh_ref, wpool_ref, pscale_ref), 0)
        scales = _group_scales(stat_refs, stat_stages)
        yield pool
        for g, dil in enumerate(DILATIONS):
            val = _natural_order(o_refs[g], dil, o_stages[g]) * scales[g]
            yield product(val, POOL_WIDTH + g * GROUP_WIDTH)

    @pl.when(s == 0)
    def _():
        carry[...] = sum(next_products())

    @pl.when(s > 0)
    def _():
        h = h_ref[0] + carry[...]
        ple = jnp.dot(p_ref[0, 0].astype(BF16), wple_ref[...], preferred_element_type=F32)
        hn = _rms(h, g2_ref[...]).astype(BF16)
        acc, nxt, products = h, 0.0, next_products()
        for c in range(0, D_FF, FF_CHUNK):
            up = jnp.dot(hn, wup_ref[:, c:c + FF_CHUNK], preferred_element_type=F32)
            act = jnp.square(jnp.maximum(up, 0.0)).astype(BF16)
            acc = acc + jnp.dot(act, wdown_ref[c:c + FF_CHUNK, :], preferred_element_type=F32)
            nxt = nxt + next(products)
        h = acc
        carry[...] = nxt

        hn = _rms(h, g3_ref[...]).astype(BF16)
        pieces = []
        for c in range(0, D, GATE_CHUNK):
            z = jnp.dot(hn, wgate_ref[:, c:c + GATE_CHUNK], preferred_element_type=F32)
            gate = 0.5 + 0.5 * jnp.tanh(0.5 * z)
            pieces.append(h[:, c:c + GATE_CHUNK] + gate * ple[:, c:c + GATE_CHUNK])
        if final:
            ssq = sum(jnp.sum(x * x, axis=-1, keepdims=True) for x in pieces)
            inv = lax.rsqrt(ssq / D + EPS)
            pieces = [x * inv * gf_ref[:, c:c + GATE_CHUNK] for x, c in zip(pieces, range(0, D, GATE_CHUNK))]
        for x, c in zip(pieces, range(0, D, GATE_CHUNK)):
            out_ref[0, :, c:c + GATE_CHUNK] = x


def _mix(h, u, o, stats, p, layer, wpool, pscale, wout, g2, wup, wdown, g3, wgate, wple, gf, final):
    B, S, D = h.shape
    tm = ROW_TILE
    nt = S // tm
    n_tiles = B * nt
    ahead = lambda s: jnp.minimum(s, n_tiles - 1)
    behind = lambda s: jnp.maximum(s - 1, 0)
    row_b = lambda width: pl.BlockSpec((1, tm, width), lambda s: (behind(s) // nt, behind(s) % nt, 0))
    row_a = lambda width: pl.BlockSpec((1, tm, width), lambda s: (ahead(s) // nt, ahead(s) % nt, 0))
    halo = pl.BlockSpec(
        (1, POOL_HALO, POOL_WIDTH),
        lambda s: (ahead(s) // nt, jnp.maximum(ahead(s) % nt * (tm // POOL_HALO) - 1, 0), 0))
    strided = [pl.BlockSpec((1, tm // dil, dil * GROUP_WIDTH), lambda s: (ahead(s) // nt, ahead(s) % nt, 0))
               for dil in DILATIONS]
    p_spec = pl.BlockSpec((1, 1, tm, p.shape[-1]), lambda s: (layer, behind(s) // nt, behind(s) % nt, 0))
    weights = [wpool, pscale, wout, g2, wup, wdown, g3, wgate, wple, gf]
    return pl.pallas_call(
        functools.partial(_mix_kernel, final=final, tiles_per_seq=nt, n_tiles=n_tiles),
        grid=(n_tiles + 1,),
        in_specs=[row_b(D), row_a(POOL_WIDTH), halo] + strided * 2 + [p_spec]
                 + [_const_spec(w.shape) for w in weights],
        out_specs=row_b(D),
        out_shape=jax.ShapeDtypeStruct((B, S, D), F32),
        scratch_shapes=[pltpu.VMEM((tm, D), F32)]
                       + [pltpu.VMEM((GROUP_WIDTH // LANES, tm, LANES), F32)
                          for dil in DILATIONS * 2 for _ in range(_stage_buffers(dil))],
        compiler_params=pltpu.CompilerParams(
            dimension_semantics=("arbitrary",), vmem_limit_bytes=VMEM_LIMIT),
        name="mix_final" if final else "mix",
    )(h, u, u, *o, *stats, p, *weights)


def _rotary_parts(positions):
    inv_freq = ROPE_THETA ** (-jnp.arange(0, ROT_DIM, 2, dtype=F32) / ROT_DIM)
    ang = positions.astype(F32)[..., None] * inv_freq
    cs = jnp.concatenate([jnp.cos(ang), jnp.sin(ang)], axis=-1)
    hi = cs.astype(BF16)
    rest = cs - hi.astype(F32)
    mid = rest.astype(BF16)
    lo = (rest - mid.astype(F32)).astype(BF16)
    lane = np.arange(LANES)
    rotary = (lane % HEAD_DIM) < ROT_DIM
    pick = ((lane[None, :] % _HALF == np.arange(_HALF)[:, None]) & rotary[None, :]).astype(np.float32)
    sign = np.where(lane < HEAD_DIM, -1.0, 1.0).astype(np.float32)
    zero = np.zeros_like(pick)
    one_part = np.block([[pick, zero], [zero, pick * sign]])
    sel = np.concatenate([one_part] * 3, axis=0)
    return jnp.concatenate([hi, mid, lo], axis=-1), jnp.asarray(sel, BF16)


def _block_diag(w):
    g, c, _ = w.shape
    eye = jnp.eye(g, dtype=w.dtype)
    return (w[:, :, None, :] * eye[:, None, :, None]).reshape(g * c, g * c)


def kernel(x, p, positions, norm1, w_in, pool_w, pool_scale, w_out, norm2, w_up, w_down, norm3, w_gate,
           w_ple, final_norm):
    depth = w_in.shape[0]
    n_grp = len(DILATIONS)
    cs, sel = _rotary_parts(positions)
    bias = _band_bias()
    vec = lambda a: a.reshape(1, -1)
    h = x
    for i in range(depth):
        u, *qkv = _inproj(h, vec(norm1[i]), _w_in_prepared(w_in, i), cs, sel)
        o, stats = [], []
        for g, dil in enumerate(DILATIONS):
            og, sg = _attention(qkv[g], qkv[n_grp + g], qkv[2 * n_grp + g], bias, dil)
            o.append(og)
            stats.append(sg)
        h = _mix(h, u, o, stats, p, i, _block_diag(pool_w[i]).astype(BF16), vec(pool_scale[i]),
                 _to_bf16(w_out, i), vec(norm2[i]), _to_bf16(w_up, i), _to_bf16(w_down, i),
                 vec(norm3[i]), _to_bf16(w_gate, i), _to_bf16(w_ple, i), vec(final_norm),
                 final=(i == depth - 1))
    return h
```

```python
import functools

import numpy as np
import jax
import jax.numpy as jnp
from jax import lax
from jax.experimental import pallas as pl
from jax.experimental.pallas import tpu as pltpu

F32 = jnp.float32
BF16 = jnp.bfloat16

D_MODEL = 1024
HEAD_DIM = 64
POOL_WIDTH = 256
POOL_WINDOWS = (2, 4, 8, 16)
POOL_GC = POOL_WIDTH // len(POOL_WINDOWS)
ATTN_WIDTH = D_MODEL - POOL_WIDTH
DILATIONS = (1, 4, 16)
GROUP_WIDTH = ATTN_WIDTH // len(DILATIONS)
ROT_DIM = HEAD_DIM // 4
ROPE_THETA = 500000.0
BLK = 128
D_FF = 4 * D_MODEL
EPS = 1e-6
MASK_VALUE = -1e30

LANES = 128
POOL_HALO = 16
ROW_TILE = 512
ATTN_ROWS = 2048
ATTN_MIN_ROWS = 512
ATTN_UNITS = 32
STAT_SPLIT = HEAD_DIM // 2
FF_CHUNK = 1024
GATE_CHUNK = 256
VMEM_LIMIT = 56 * 1024 * 1024
WEIGHT_SLABS = 8
SUBLANE_STRIDE = 4


def _rms(x, g):
    return x * lax.rsqrt(jnp.mean(x * x, axis=-1, keepdims=True) + EPS) * g


def _const_spec(shape, layer=None):
    zeros = (0,) * len(shape)
    if layer is None:
        return pl.BlockSpec(shape, lambda *_: zeros, pipeline_mode=pl.Buffered(1))
    return pl.BlockSpec((None,) + tuple(shape), lambda *_: (layer,) + zeros, pipeline_mode=pl.Buffered(1))


_HALF = ROT_DIM // 2
_QK_TILE_PERM = np.concatenate([
    np.arange(0, _HALF), np.arange(HEAD_DIM, HEAD_DIM + _HALF), np.arange(ROT_DIM, HEAD_DIM),
    np.arange(_HALF, ROT_DIM), np.arange(HEAD_DIM + _HALF, HEAD_DIM + ROT_DIM),
    np.arange(HEAD_DIM + ROT_DIM, LANES)])
_QKV_ORDER = ((0, 2), (0, 0), (1, 2), (0, 1), (2, 2), (1, 0), (1, 1), (2, 0), (2, 1))


def _residue_rows(src, mid, dil):
    tm = src.shape[0]
    if dil <= SUBLANE_STRIDE:
        for r in range(dil):
            yield r, src[pl.ds(r, tm // dil, stride=dil), :]
        return
    first, second = SUBLANE_STRIDE, dil // SUBLANE_STRIDE
    part = tm // first
    for rl in range(first):
        mid[rl * part:(rl + 1) * part, :] = src[pl.ds(rl, part, stride=first), :]
    for r in range(dil):
        rh, rl = divmod(r, first)
        yield r, mid[pl.ds(rl * part + rh, tm // dil, stride=second), :]


def _prep_kernel(*refs):
    n = len(refs) // 2
    (win_ref, *plain_in), (win_out, *plain_out) = refs[:n], refs[n:]
    for w_ref, o_ref in zip(plain_in, plain_out):
        o_ref[0] = w_ref[0].astype(o_ref.dtype)
    lane = lax.broadcasted_iota(jnp.int32, (win_ref.shape[1], LANES), 1)
    swap = HEAD_DIM - _HALF
    from_high = (lane >= _HALF) & (lane < ROT_DIM)
    from_low = (lane >= HEAD_DIM) & (lane < HEAD_DIM + _HALF)
    for c in range(0, win_ref.shape[2], LANES):
        w = win_ref[0, :, c:c + LANES]
        if POOL_WIDTH <= c < POOL_WIDTH + 2 * ATTN_WIDTH:
            w = jnp.where(from_high, pltpu.roll(w, LANES - swap, 1), jnp.where(from_low, pltpu.roll(w, swap, 1), w))
        if POOL_WIDTH <= c < POOL_WIDTH + ATTN_WIDTH:
            w = w * HEAD_DIM ** -0.5
        win_out[0, :, c:c + LANES] = w.astype(win_out.dtype)


def _prepare_weights(w_in, *plain):
    weights = (w_in,) + plain
    depth = w_in.shape[0]
    slab = lambda w: pl.BlockSpec((1, w.shape[1] // WEIGHT_SLABS, w.shape[2]), lambda l, i: (l, i, 0))
    return pl.pallas_call(
        _prep_kernel,
        grid=(depth, WEIGHT_SLABS),
        in_specs=[slab(w) for w in weights],
        out_specs=[slab(w) for w in weights],
        out_shape=[jax.ShapeDtypeStruct(w.shape, BF16) for w in weights],
        compiler_params=pltpu.CompilerParams(
            dimension_semantics=("parallel", "parallel"), vmem_limit_bytes=VMEM_LIMIT),
        name="weight_prep",
    )(*weights)


def _head_a_lanes(lane):
    return (lane < _HALF) | ((lane >= ROT_DIM) & (lane < HEAD_DIM + _HALF))


def _inproj_kernel(x_ref, g_ref, w_ref, cs_ref, sel_ref, u_ref, *refs):
    n_grp = len(DILATIONS)
    qkv_refs, (hnbuf, zbuf, *stages) = refs[:3 * n_grp], refs[3 * n_grp:]
    tm = x_ref.shape[1]
    s = pl.program_id(0)

    def column(kind, g):
        return POOL_WIDTH + kind * ATTN_WIDTH + g * GROUP_WIDTH

    def prepare_next():
        hn_next = _rms(x_ref[0], g_ref[...]).astype(BF16)
        first = column(*_QKV_ORDER[0])
        zbuf[...] = jnp.dot(hn_next, w_ref[:, first:first + GROUP_WIDTH], preferred_element_type=F32)
        hnbuf[...] = hn_next

    @pl.when(s == 0)
    def _():
        prepare_next()

    @pl.when(s > 0)
    def _():
        table = jnp.dot(cs_ref[0], sel_ref[...], preferred_element_type=F32)
        lane = lax.broadcasted_iota(jnp.int32, (1, LANES), 1)
        cos_t = table[:, :LANES] + ((lane % HEAD_DIM) >= ROT_DIM).astype(F32)
        sin_t = table[:, LANES:]
        hn = hnbuf[...]

        for b, (kind, g) in enumerate(_QKV_ORDER):
            out_ref = qkv_refs[kind * n_grp + g]
            if b == 0:
                z = zbuf[...]
            else:
                col = column(kind, g)
                z = jnp.dot(hn, w_ref[:, col:col + GROUP_WIDTH], preferred_element_type=F32)
            halves = [z[:, c:c + LANES] for c in range(0, GROUP_WIDTH, LANES)]
            if kind < 2:
                halves = [z_h * cos_t + pltpu.roll(z_h, LANES // 2, 1) * sin_t for z_h in halves]
            dil = DILATIONS[g]
            if dil == 1:
                out_ref[0] = jnp.concatenate(halves, axis=-1).astype(out_ref.dtype)
                continue
            src, mid = stages[2 * (b % 2)], stages[2 * (b % 2) + 1]
            for t, z_h in enumerate(halves):
                src[t] = z_h
                for r, rows in _residue_rows(src.at[t], mid.at[t], dil):
                    c = r * GROUP_WIDTH + t * LANES
                    out_ref[0, :, c:c + LANES] = rows.astype(out_ref.dtype)
        u_ref[0] = jnp.dot(hn, w_ref[:, :POOL_WIDTH], preferred_element_type=F32)
        prepare_next()


def _inproj(x, g, w_in, layer, cs, sel):
    B, S, D = x.shape
    n_in = w_in.shape[2]
    tm = ROW_TILE
    nt = S // tm
    n_tiles = B * nt
    ahead = lambda s: jnp.minimum(s, n_tiles - 1)
    behind = lambda s: jnp.maximum(s - 1, 0)
    row = lambda width, tile: pl.BlockSpec((1, tm, width), lambda s: (tile(s) // nt, tile(s) % nt, 0))
    strided = lambda dil: pl.BlockSpec((1, tm // dil, dil * GROUP_WIDTH),
                                       lambda s: (behind(s) // nt, behind(s) % nt, 0))
    out_shape = [jax.ShapeDtypeStruct((B, S, POOL_WIDTH), F32)]
    out_shape += [jax.ShapeDtypeStruct((B, S // dil, dil * GROUP_WIDTH), BF16) for dil in DILATIONS] * 3
    return pl.pallas_call(
        _inproj_kernel,
        grid=(n_tiles + 1,),
        in_specs=[row(D, ahead), _const_spec((1, D)), _const_spec((D, n_in), layer), row(cs.shape[-1], behind),
                  _const_spec(sel.shape)],
        out_specs=[row(POOL_WIDTH, behind)] + [strided(dil) for dil in DILATIONS] * 3,
        out_shape=out_shape,
        scratch_shapes=[pltpu.VMEM((tm, D), BF16), pltpu.VMEM((tm, GROUP_WIDTH), F32)]
                       + [pltpu.VMEM((GROUP_WIDTH // LANES, tm, LANES), F32)] * 4,
        compiler_params=pltpu.CompilerParams(
            dimension_semantics=("arbitrary",), vmem_limit_bytes=VMEM_LIMIT),
        name="inproj",
    )(x, g, w_in, cs, sel)


def _attn_kernel(q_ref, k_ref, kh_ref, v_ref, vh_ref, bias_ref, o_ref, stat_ref, kbuf, vbuf):
    rows, cols = q_ref.shape[1], q_ref.shape[2]
    n_blk = rows // BLK
    first_chunk = pl.program_id(1) == 0

    kbuf[0:BLK] = kh_ref[0]
    kbuf[BLK:] = k_ref[0]
    vbuf[0:BLK] = vh_ref[0]
    vbuf[BLK:] = v_ref[0]

    lane = lax.broadcasted_iota(jnp.int32, (BLK, LANES), 1)
    head0 = lane < HEAD_DIM
    head0_qk = _head_a_lanes(lane)
    stat_is_max = (lane % HEAD_DIM) < STAT_SPLIT
    ones = jnp.ones((2 * BLK, LANES), BF16)

    def unit(i, c):
        r0 = pl.multiple_of(i * BLK, BLK)
        q = q_ref[0, pl.ds(r0, BLK), c:c + LANES]
        kwin = kbuf[pl.ds(r0, 2 * BLK), c:c + LANES]
        vwin = vbuf[pl.ds(r0, 2 * BLK), c:c + LANES]
        zero = jnp.zeros_like(q)
        q2 = jnp.concatenate([jnp.where(head0_qk, q, zero), jnp.where(head0_qk, zero, q)], axis=0)
        s = lax.dot_general(q2, kwin, (((1,), (1,)), ((), ())), preferred_element_type=F32)
        no_history = jnp.logical_and(first_chunk, i == 0).astype(jnp.int32)
        s = s + bias_ref[no_history]
        m = jnp.max(s, axis=-1, keepdims=True)
        e = jnp.exp(s - m).astype(BF16)
        pv = jnp.dot(e, jnp.concatenate([vwin, ones], axis=1), preferred_element_type=F32)
        o_ref[0, pl.ds(r0, BLK), c:c + LANES] = jnp.where(
            head0, pv[:BLK, :LANES], pv[BLK:, :LANES]).astype(o_ref.dtype)
        m_both = jnp.where(head0, jnp.broadcast_to(m[:BLK], (BLK, LANES)), jnp.broadcast_to(m[BLK:], (BLK, LANES)))
        l_both = jnp.where(head0, pv[:BLK, LANES:], pv[BLK:, LANES:])
        stat_ref[0, pl.ds(r0, BLK), c:c + LANES] = jnp.where(stat_is_max, m_both, l_both)

    tiles = cols // LANES
    tiles_per_body = min(tiles, ATTN_UNITS)
    blocks_per_body = ATTN_UNITS // tiles_per_body
    for c0 in range(0, tiles, tiles_per_body):
        def body(ib, carry, c0=c0):
            for bi in range(blocks_per_body):
                for ci in range(tiles_per_body):
                    unit(ib * blocks_per_body + bi, (c0 + ci) * LANES)
            return carry
        lax.fori_loop(0, n_blk // blocks_per_body, body, 0)


def _attention(q, k, v, bias, dil):
    B, L, width = q.shape
    W = width // dil
    rows = min(L, max(ATTN_ROWS // dil, ATTN_MIN_ROWS))
    cols = (ATTN_ROWS // rows) * W
    hist_blocks = rows // BLK
    main = pl.BlockSpec((1, rows, cols), lambda b, n, r: (b, n, r))
    hist = pl.BlockSpec((1, BLK, cols), lambda b, n, r: (b, jnp.maximum(n * hist_blocks - 1, 0), r))
    return pl.pallas_call(
        _attn_kernel,
        grid=(B, L // rows, dil * W // cols),
        in_specs=[main, main, hist, main, hist, _const_spec(bias.shape)],
        out_specs=[main, main],
        out_shape=[jax.ShapeDtypeStruct((B, L, dil * W), BF16), jax.ShapeDtypeStruct((B, L, dil * W), F32)],
        scratch_shapes=[pltpu.VMEM((BLK + rows, cols), BF16), pltpu.VMEM((BLK + rows, cols), BF16)],
        compiler_params=pltpu.CompilerParams(
            dimension_semantics=("parallel", "arbitrary", "parallel"), vmem_limit_bytes=VMEM_LIMIT),
        name=f"attn_d{dil}",
    )(q, k, k, v, v, bias)


def _band_bias():
    a = np.arange(2 * BLK)[:, None] % BLK
    c = np.arange(2 * BLK)[None, :]
    band = (c >= a) & (c <= a + BLK)
    both = np.stack([band, band & (c >= BLK)])
    return jnp.asarray(np.where(both, 0.0, MASK_VALUE), F32)


def _stage_buffers(dil):
    return 0 if dil == 1 else 1 if dil <= SUBLANE_STRIDE else 2


def _natural_order(ref, dil, stage):
    if dil == 1:
        return ref[0].astype(F32)
    n = ref.shape[1]
    tm = n * dil
    tiles = GROUP_WIDTH // LANES
    dst, mid = stage
    first = min(dil, SUBLANE_STRIDE)
    second, part = dil // first, tm // first
    for t in range(tiles):
        for r in range(dil):
            rh, rl = divmod(r, first)
            c = r * GROUP_WIDTH + t * LANES
            rows = ref[0, :, c:c + LANES].astype(F32)
            if second == 1:
                dst[t, pl.ds(r, n, stride=dil), :] = rows
            else:
                mid[t, pl.ds(rl * part + rh, n, stride=second), :] = rows
        if second > 1:
            for rl in range(first):
                dst[t, pl.ds(rl, part, stride=first), :] = mid[t, rl * part:(rl + 1) * part, :]
    return jnp.concatenate([dst[t] for t in range(tiles)], axis=-1)


def _pool_mixer(tile_in_seq, u_ref, uh_ref, wpool_ref, pscale_ref):
    tm = u_ref.shape[1]
    hist = jnp.where(tile_in_seq == 0, 0.0, uh_ref[0])
    ub = jnp.concatenate([hist, u_ref[0]], axis=0)
    s2 = ub + pltpu.roll(ub, 1, 0)
    s4 = s2 + pltpu.roll(s2, 2, 0)
    s8 = s4 + pltpu.roll(s4, 4, 0)
    s16 = s8 + pltpu.roll(s8, 8, 0)
    t = lax.broadcasted_iota(jnp.int32, (tm, LANES), 0) + tile_in_seq * tm + 1
    lane = lax.broadcasted_iota(jnp.int32, (tm, LANES), 1)
    low = lane < POOL_GC
    cnt = lambda w: jnp.minimum(t, w).astype(F32)
    tile = lambda s, c: s[POOL_HALO:, c:c + LANES]
    y_lo = jnp.where(low, tile(s2, 0) / cnt(2), tile(s4, 0) / cnt(4)) - tile(ub, 0)
    y_hi = jnp.where(low, tile(s8, LANES) / cnt(8), tile(s16, LANES) / cnt(16)) - tile(ub, LANES)
    y = jnp.concatenate([y_lo, y_hi], axis=-1).astype(BF16)
    return jnp.dot(y, wpool_ref[...], preferred_element_type=F32) * pscale_ref[...]


def _group_scales(stat_refs, stages):
    stat = [_natural_order(r, d, stage) for r, d, stage in zip(stat_refs, DILATIONS, stages)]
    lane = lax.broadcasted_iota(jnp.int32, stat[0].shape, 1)
    is_max = (lane % HEAD_DIM) < STAT_SPLIT
    mx = jnp.maximum(jnp.maximum(stat[0], stat[1]), stat[2])
    e_on_max = [jnp.exp(s - mx) for s in stat]
    e_on_sum = [pltpu.roll(e, STAT_SPLIT, 1) for e in e_on_max]
    inv_on_sum = 1.0 / (stat[0] * e_on_sum[0] + stat[1] * e_on_sum[1] + stat[2] * e_on_sum[2])
    inv_on_max = pltpu.roll(inv_on_sum, GROUP_WIDTH - STAT_SPLIT, 1)
    return [jnp.where(is_max, a * inv_on_max, b * inv_on_sum) for a, b in zip(e_on_max, e_on_sum)]


def _mix_kernel(h_ref, u_ref, uh_ref, o1_ref, o4_ref, o16_ref, l1_ref, l4_ref, l16_ref, p_ref,
                wpool_ref, pscale_ref, wout_ref, g2_ref, wup_ref, wdown_ref, g3_ref, wgate_ref, wple_ref,
                gf_ref, out_ref, carry, *stages, final, tiles_per_seq, n_tiles):
    s = pl.program_id(0)
    D = h_ref.shape[2]
    tile_in_seq = jnp.minimum(s, n_tiles - 1) % tiles_per_seq
    o_refs, stat_refs = (o1_ref, o4_ref, o16_ref), (l1_ref, l4_ref, l16_ref)
    stage = iter(stages)
    take = lambda d: [next(stage) for _ in range(_stage_buffers(d))] + [None] * (2 - _stage_buffers(d))
    o_stages = [take(d) for d in DILATIONS]
    stat_stages = [take(d) for d in DILATIONS]

    def next_products():
        def product(val, row):
            return jnp.dot(val.astype(BF16), wout_ref[row:row + GROUP_WIDTH, :], preferred_element_type=F32)
        pool = product(_pool_mixer(tile_in_seq, u_ref, uh_ref, wpool_ref, pscale_ref), 0)
        scales = _group_scales(stat_refs, stat_stages)
        yield pool
        for g, dil in enumerate(DILATIONS):
            val = _natural_order(o_refs[g], dil, o_stages[g]) * scales[g]
            yield product(val, POOL_WIDTH + g * GROUP_WIDTH)

    @pl.when(s == 0)
    def _():
        carry[...] = sum(next_products())

    @pl.when(s > 0)
    def _():
        h = h_ref[0] + carry[...]
        ple = jnp.dot(p_ref[0, 0].astype(BF16), wple_ref[...], preferred_element_type=F32)
        hn = _rms(h, g2_ref[...]).astype(BF16)
        acc, nxt, products = h, 0.0, next_products()
        for c in range(0, D_FF, FF_CHUNK):
            up = jnp.dot(hn, wup_ref[:, c:c + FF_CHUNK], preferred_element_type=F32)
            act = jnp.square(jnp.maximum(up, 0.0)).astype(BF16)
            acc = acc + jnp.dot(act, wdown_ref[c:c + FF_CHUNK, :], preferred_element_type=F32)
            nxt = nxt + next(products)
        h = acc
        carry[...] = nxt

        hn = _rms(h, g3_ref[...]).astype(BF16)
        pieces = []
        for c in range(0, D, GATE_CHUNK):
            z = jnp.dot(hn, wgate_ref[:, c:c + GATE_CHUNK], preferred_element_type=F32)
            gate = 0.5 + 0.5 * jnp.tanh(0.5 * z)
            pieces.append(h[:, c:c + GATE_CHUNK] + gate * ple[:, c:c + GATE_CHUNK])
        if final:
            ssq = sum(jnp.sum(x * x, axis=-1, keepdims=True) for x in pieces)
            inv = lax.rsqrt(ssq / D + EPS)
            pieces = [x * inv * gf_ref[:, c:c + GATE_CHUNK] for x, c in zip(pieces, range(0, D, GATE_CHUNK))]
        for x, c in zip(pieces, range(0, D, GATE_CHUNK)):
            out_ref[0, :, c:c + GATE_CHUNK] = x


def _mix(h, u, o, stats, p, layer, wpool, pscale, wout, g2, wup, wdown, g3, wgate, wple, gf, final):
    B, S, D = h.shape
    tm = ROW_TILE
    nt = S // tm
    n_tiles = B * nt
    ahead = lambda s: jnp.minimum(s, n_tiles - 1)
    behind = lambda s: jnp.maximum(s - 1, 0)
    row_b = lambda width: pl.BlockSpec((1, tm, width), lambda s: (behind(s) // nt, behind(s) % nt, 0))
    row_a = lambda width: pl.BlockSpec((1, tm, width), lambda s: (ahead(s) // nt, ahead(s) % nt, 0))
    halo = pl.BlockSpec(
        (1, POOL_HALO, POOL_WIDTH),
        lambda s: (ahead(s) // nt, jnp.maximum(ahead(s) % nt * (tm // POOL_HALO) - 1, 0), 0))
    strided = [pl.BlockSpec((1, tm // dil, dil * GROUP_WIDTH), lambda s: (ahead(s) // nt, ahead(s) % nt, 0))
               for dil in DILATIONS]
    p_spec = pl.BlockSpec((1, 1, tm, p.shape[-1]), lambda s: (layer, behind(s) // nt, behind(s) % nt, 0))
    weights = [wpool, pscale, wout, g2, wup, wdown, g3, wgate, wple, gf]
    return pl.pallas_call(
        functools.partial(_mix_kernel, final=final, tiles_per_seq=nt, n_tiles=n_tiles),
        grid=(n_tiles + 1,),
        in_specs=[row_b(D), row_a(POOL_WIDTH), halo] + strided * 2 + [p_spec]
                 + [_const_spec(w.shape[1:], layer) if w.ndim == 3 else _const_spec(w.shape) for w in weights],
        out_specs=row_b(D),
        out_shape=jax.ShapeDtypeStruct((B, S, D), F32),
        scratch_shapes=[pltpu.VMEM((tm, D), F32)]
                       + [pltpu.VMEM((GROUP_WIDTH // LANES, tm, LANES), F32)
                          for dil in DILATIONS * 2 for _ in range(_stage_buffers(dil))],
        compiler_params=pltpu.CompilerParams(
            dimension_semantics=("arbitrary",), vmem_limit_bytes=VMEM_LIMIT),
        name="mix_final" if final else "mix",
    )(h, u, u, *o, *stats, p, *weights)


def _rotary_parts(positions):
    inv_freq = ROPE_THETA ** (-jnp.arange(0, ROT_DIM, 2, dtype=F32) / ROT_DIM)
    ang = positions.astype(F32)[..., None] * inv_freq
    cs = jnp.concatenate([jnp.cos(ang), jnp.sin(ang)], axis=-1)
    hi = cs.astype(BF16)
    rest = cs - hi.astype(F32)
    mid = rest.astype(BF16)
    lo = (rest - mid.astype(F32)).astype(BF16)
    lane = np.arange(LANES)
    rotary = (lane % HEAD_DIM) < ROT_DIM
    pick = ((lane[None, :] % _HALF == np.arange(_HALF)[:, None]) & rotary[None, :]).astype(np.float32)
    sign = np.where(lane < HEAD_DIM, -1.0, 1.0).astype(np.float32)
    zero = np.zeros_like(pick)
    one_part = np.block([[pick, zero], [zero, pick * sign]])
    sel = np.concatenate([one_part] * 3, axis=0)
    return jnp.concatenate([hi, mid, lo], axis=-1), jnp.asarray(sel, BF16)


def _block_diag(w):
    g, c, _ = w.shape
    eye = jnp.eye(g, dtype=w.dtype)
    return (w[:, :, None, :] * eye[:, None, :, None]).reshape(g * c, g * c)


def kernel(x, p, positions, norm1, w_in, pool_w, pool_scale, w_out, norm2, w_up, w_down, norm3, w_gate,
           w_ple, final_norm):
    depth = w_in.shape[0]
    n_grp = len(DILATIONS)
    cs, sel = _rotary_parts(positions)
    bias = _band_bias()
    vec = lambda a: a.reshape(1, -1)
    w_in_b, w_out_b, w_up_b, w_down_b, w_gate_b, w_ple_b = _prepare_weights(
        w_in, w_out, w_up, w_down, w_gate, w_ple)
    h = x
    for i in range(depth):
        u, *qkv = _inproj(h, vec(norm1[i]), w_in_b, i, cs, sel)
        o, stats = [], []
        for g, dil in enumerate(DILATIONS):
            og, sg = _attention(qkv[g], qkv[n_grp + g], qkv[2 * n_grp + g], bias, dil)
            o.append(og)
            stats.append(sg)
        h = _mix(h, u, o, stats, p, i, _block_diag(pool_w[i]).astype(BF16), vec(pool_scale[i]),
                 w_out_b, vec(norm2[i]), w_up_b, w_down_b, vec(norm3[i]), w_gate_b, w_ple_b, vec(final_norm),
                 final=(i == depth - 1))
    return h
```

```python
import functools

import numpy as np
import jax
import jax.numpy as jnp
from jax import lax
from jax.experimental import pallas as pl
from jax.experimental.pallas import tpu as pltpu

F32 = jnp.float32
BF16 = jnp.bfloat16

D_MODEL = 1024
HEAD_DIM = 64
POOL_WIDTH = 256
POOL_WINDOWS = (2, 4, 8, 16)
POOL_GC = POOL_WIDTH // len(POOL_WINDOWS)
ATTN_WIDTH = D_MODEL - POOL_WIDTH
DILATIONS = (1, 4, 16)
GROUP_WIDTH = ATTN_WIDTH // len(DILATIONS)
ROT_DIM = HEAD_DIM // 4
ROPE_THETA = 500000.0
BLK = 128
D_FF = 4 * D_MODEL
EPS = 1e-6
MASK_VALUE = -1e30

LANES = 128
POOL_HALO = 16
ROW_TILE = 512
INPROJ_TILE = 1024
ATTN_ROWS = 2048
ATTN_MIN_ROWS = 512
ATTN_UNITS = 32
STAT_SPLIT = HEAD_DIM // 2
FF_CHUNK = 1024
GATE_CHUNK = 256
VMEM_LIMIT = 56 * 1024 * 1024
WEIGHT_SLABS = 8
SUBLANE_STRIDE = 4


def _rms(x, g):
    return x * lax.rsqrt(jnp.mean(x * x, axis=-1, keepdims=True) + EPS) * g


def _const_spec(shape, layer=None):
    zeros = (0,) * len(shape)
    if layer is None:
        return pl.BlockSpec(shape, lambda *_: zeros, pipeline_mode=pl.Buffered(1))
    return pl.BlockSpec((None,) + tuple(shape), lambda *_: (layer,) + zeros, pipeline_mode=pl.Buffered(1))


_HALF = ROT_DIM // 2
_QK_TILE_PERM = np.concatenate([
    np.arange(0, _HALF), np.arange(HEAD_DIM, HEAD_DIM + _HALF), np.arange(ROT_DIM, HEAD_DIM),
    np.arange(_HALF, ROT_DIM), np.arange(HEAD_DIM + _HALF, HEAD_DIM + ROT_DIM),
    np.arange(HEAD_DIM + ROT_DIM, LANES)])
_QKV_ORDER = ((0, 2), (0, 0), (1, 2), (0, 1), (2, 2), (1, 0), (1, 1), (2, 0), (2, 1))


def _residue_rows(src, mid, dil):
    tm = src.shape[0]
    if dil <= SUBLANE_STRIDE:
        for r in range(dil):
            yield r, src[pl.ds(r, tm // dil, stride=dil), :]
        return
    first, second = SUBLANE_STRIDE, dil // SUBLANE_STRIDE
    part = tm // first
    for rl in range(first):
        mid[rl * part:(rl + 1) * part, :] = src[pl.ds(rl, part, stride=first), :]
    for r in range(dil):
        rh, rl = divmod(r, first)
        yield r, mid[pl.ds(rl * part + rh, tm // dil, stride=second), :]


def _prep_kernel(*refs, scales):
    n = len(refs) // 2
    (win_ref, *plain_in), (win_out, *plain_out) = refs[:n], refs[n:]
    for w_ref, o_ref, scale in zip(plain_in, plain_out, scales):
        w = w_ref[0]
        o_ref[0] = (w if scale == 1.0 else w * scale).astype(o_ref.dtype)
    lane = lax.broadcasted_iota(jnp.int32, (win_ref.shape[1], LANES), 1)
    swap = HEAD_DIM - _HALF
    from_high = (lane >= _HALF) & (lane < ROT_DIM)
    from_low = (lane >= HEAD_DIM) & (lane < HEAD_DIM + _HALF)
    for c in range(0, win_ref.shape[2], LANES):
        w = win_ref[0, :, c:c + LANES]
        if POOL_WIDTH <= c < POOL_WIDTH + 2 * ATTN_WIDTH:
            w = jnp.where(from_high, pltpu.roll(w, LANES - swap, 1), jnp.where(from_low, pltpu.roll(w, swap, 1), w))
        if POOL_WIDTH <= c < POOL_WIDTH + ATTN_WIDTH:
            w = w * HEAD_DIM ** -0.5
        win_out[0, :, c:c + LANES] = w.astype(win_out.dtype)


def _prepare_weights(w_in, plain, scales):
    weights = (w_in,) + tuple(plain)
    depth = w_in.shape[0]
    slab = lambda w: pl.BlockSpec((1, w.shape[1] // WEIGHT_SLABS, w.shape[2]), lambda l, i: (l, i, 0))
    return pl.pallas_call(
        functools.partial(_prep_kernel, scales=tuple(scales)),
        grid=(depth, WEIGHT_SLABS),
        in_specs=[slab(w) for w in weights],
        out_specs=[slab(w) for w in weights],
        out_shape=[jax.ShapeDtypeStruct(w.shape, BF16) for w in weights],
        compiler_params=pltpu.CompilerParams(
            dimension_semantics=("parallel", "parallel"), vmem_limit_bytes=VMEM_LIMIT),
        name="weight_prep",
    )(*weights)


def _head_a_lanes(lane):
    return (lane < _HALF) | ((lane >= ROT_DIM) & (lane < HEAD_DIM + _HALF))


def _inproj_kernel(x_ref, g_ref, w_ref, cs_ref, sel_ref, u_ref, *refs):
    n_grp = len(DILATIONS)
    qkv_refs, (hnbuf, zbuf, *stages) = refs[:3 * n_grp], refs[3 * n_grp:]
    tm = x_ref.shape[1]
    s = pl.program_id(0)

    def column(kind, g):
        return POOL_WIDTH + kind * ATTN_WIDTH + g * GROUP_WIDTH

    def prepare_next():
        hn_next = _rms(x_ref[0], g_ref[...]).astype(BF16)
        first = column(*_QKV_ORDER[0])
        zbuf[...] = jnp.dot(hn_next, w_ref[:, first:first + GROUP_WIDTH], preferred_element_type=F32)
        hnbuf[...] = hn_next

    @pl.when(s == 0)
    def _():
        prepare_next()

    @pl.when(s > 0)
    def _():
        table = jnp.dot(cs_ref[0], sel_ref[...], preferred_element_type=F32)
        lane = lax.broadcasted_iota(jnp.int32, (1, LANES), 1)
        cos_t = table[:, :LANES] + ((lane % HEAD_DIM) >= ROT_DIM).astype(F32)
        sin_t = table[:, LANES:]
        hn = hnbuf[...]

        for b, (kind, g) in enumerate(_QKV_ORDER):
            out_ref = qkv_refs[kind * n_grp + g]
            if b == 0:
                z = zbuf[...]
            else:
                col = column(kind, g)
                z = jnp.dot(hn, w_ref[:, col:col + GROUP_WIDTH], preferred_element_type=F32)
            halves = [z[:, c:c + LANES] for c in range(0, GROUP_WIDTH, LANES)]
            if kind < 2:
                halves = [z_h * cos_t + pltpu.roll(z_h, LANES // 2, 1) * sin_t for z_h in halves]
            dil = DILATIONS[g]
            if dil == 1:
                out_ref[0] = jnp.concatenate(halves, axis=-1).astype(out_ref.dtype)
                continue
            src, mid = stages[2 * (b % 2)], stages[2 * (b % 2) + 1]
            for t, z_h in enumerate(halves):
                src[t] = z_h
                for r, rows in _residue_rows(src.at[t], mid.at[t], dil):
                    c = r * GROUP_WIDTH + t * LANES
                    out_ref[0, :, c:c + LANES] = rows.astype(out_ref.dtype)
        u_ref[0] = jnp.dot(hn, w_ref[:, :POOL_WIDTH], preferred_element_type=F32)
        prepare_next()


def _inproj(x, g, w_in, layer, cs, sel):
    B, S, D = x.shape
    n_in = w_in.shape[2]
    tm = INPROJ_TILE
    nt = S // tm
    n_tiles = B * nt
    ahead = lambda s: jnp.minimum(s, n_tiles - 1)
    behind = lambda s: jnp.maximum(s - 1, 0)
    row = lambda width, tile: pl.BlockSpec((1, tm, width), lambda s: (tile(s) // nt, tile(s) % nt, 0))
    strided = lambda dil: pl.BlockSpec((1, tm // dil, dil * GROUP_WIDTH),
                                       lambda s: (behind(s) // nt, behind(s) % nt, 0))
    out_shape = [jax.ShapeDtypeStruct((B, S, POOL_WIDTH), F32)]
    out_shape += [jax.ShapeDtypeStruct((B, S // dil, dil * GROUP_WIDTH), BF16) for dil in DILATIONS] * 3
    return pl.pallas_call(
        _inproj_kernel,
        grid=(n_tiles + 1,),
        in_specs=[row(D, ahead), _const_spec((1, D)), _const_spec((D, n_in), layer), row(cs.shape[-1], behind),
                  _const_spec(sel.shape)],
        out_specs=[row(POOL_WIDTH, behind)] + [strided(dil) for dil in DILATIONS] * 3,
        out_shape=out_shape,
        scratch_shapes=[pltpu.VMEM((tm, D), BF16), pltpu.VMEM((tm, GROUP_WIDTH), F32)]
                       + [pltpu.VMEM((GROUP_WIDTH // LANES, tm, LANES), F32)] * 4,
        compiler_params=pltpu.CompilerParams(
            dimension_semantics=("arbitrary",), vmem_limit_bytes=VMEM_LIMIT),
        name="inproj",
    )(x, g, w_in, cs, sel)


def _attn_kernel(q_ref, k_ref, kh_ref, v_ref, vh_ref, bias_ref, o_ref, stat_ref, kbuf, vbuf):
    rows, cols = q_ref.shape[1], q_ref.shape[2]
    n_blk = rows // BLK
    first_chunk = pl.program_id(1) == 0

    kbuf[0:BLK] = kh_ref[0]
    kbuf[BLK:] = k_ref[0]
    vbuf[0:BLK] = vh_ref[0]
    vbuf[BLK:] = v_ref[0]

    lane = lax.broadcasted_iota(jnp.int32, (BLK, LANES), 1)
    head0 = lane < HEAD_DIM
    head0_qk = _head_a_lanes(lane)
    stat_is_max = (lane % HEAD_DIM) < STAT_SPLIT
    ones = jnp.ones((2 * BLK, LANES), BF16)

    def unit(i, c):
        r0 = pl.multiple_of(i * BLK, BLK)
        q = q_ref[0, pl.ds(r0, BLK), c:c + LANES]
        kwin = kbuf[pl.ds(r0, 2 * BLK), c:c + LANES]
        vwin = vbuf[pl.ds(r0, 2 * BLK), c:c + LANES]
        zero = jnp.zeros_like(q)
        q2 = jnp.concatenate([jnp.where(head0_qk, q, zero), jnp.where(head0_qk, zero, q)], axis=0)
        s = lax.dot_general(q2, kwin, (((1,), (1,)), ((), ())), preferred_element_type=F32)
        no_history = jnp.logical_and(first_chunk, i == 0).astype(jnp.int32)
        s = s + bias_ref[no_history]
        m = jnp.max(s, axis=-1, keepdims=True)
        e = jnp.exp(s - m).astype(BF16)
        pv = jnp.dot(e, jnp.concatenate([vwin, ones], axis=1), preferred_element_type=F32)
        o_ref[0, pl.ds(r0, BLK), c:c + LANES] = jnp.where(
            head0, pv[:BLK, :LANES], pv[BLK:, :LANES]).astype(o_ref.dtype)
        m_both = jnp.where(head0, jnp.broadcast_to(m[:BLK], (BLK, LANES)), jnp.broadcast_to(m[BLK:], (BLK, LANES)))
        l_both = jnp.where(head0, pv[:BLK, LANES:], pv[BLK:, LANES:])
        stat_ref[0, pl.ds(r0, BLK), c:c + LANES] = jnp.where(stat_is_max, m_both, l_both)

    tiles = cols // LANES
    tiles_per_body = min(tiles, ATTN_UNITS)
    blocks_per_body = ATTN_UNITS // tiles_per_body
    for c0 in range(0, tiles, tiles_per_body):
        def body(ib, carry, c0=c0):
            for bi in range(blocks_per_body):
                for ci in range(tiles_per_body):
                    unit(ib * blocks_per_body + bi, (c0 + ci) * LANES)
            return carry
        lax.fori_loop(0, n_blk // blocks_per_body, body, 0)


def _attention(q, k, v, bias, dil):
    B, L, width = q.shape
    W = width // dil
    rows = min(L, max(ATTN_ROWS // dil, ATTN_MIN_ROWS))
    cols = (ATTN_ROWS // rows) * W
    hist_blocks = rows // BLK
    main = pl.BlockSpec((1, rows, cols), lambda b, n, r: (b, n, r))
    hist = pl.BlockSpec((1, BLK, cols), lambda b, n, r: (b, jnp.maximum(n * hist_blocks - 1, 0), r))
    return pl.pallas_call(
        _attn_kernel,
        grid=(B, L // rows, dil * W // cols),
        in_specs=[main, main, hist, main, hist, _const_spec(bias.shape)],
        out_specs=[main, main],
        out_shape=[jax.ShapeDtypeStruct((B, L, dil * W), BF16), jax.ShapeDtypeStruct((B, L, dil * W), F32)],
        scratch_shapes=[pltpu.VMEM((BLK + rows, cols), BF16), pltpu.VMEM((BLK + rows, cols), BF16)],
        compiler_params=pltpu.CompilerParams(
            dimension_semantics=("parallel", "arbitrary", "parallel"), vmem_limit_bytes=VMEM_LIMIT),
        name=f"attn_d{dil}",
    )(q, k, k, v, v, bias)


def _band_bias():
    a = np.arange(2 * BLK)[:, None] % BLK
    c = np.arange(2 * BLK)[None, :]
    band = (c >= a) & (c <= a + BLK)
    both = np.stack([band, band & (c >= BLK)])
    return jnp.asarray(np.where(both, 0.0, MASK_VALUE), F32)


def _stage_buffers(dil):
    return 0 if dil == 1 else 1 if dil <= SUBLANE_STRIDE else 2


def _natural_order(ref, dil, stage):
    if dil == 1:
        return ref[0].astype(F32)
    n = ref.shape[1]
    tm = n * dil
    tiles = GROUP_WIDTH // LANES
    dst, mid = stage
    first = min(dil, SUBLANE_STRIDE)
    second, part = dil // first, tm // first
    for t in range(tiles):
        for r in range(dil):
            rh, rl = divmod(r, first)
            c = r * GROUP_WIDTH + t * LANES
            rows = ref[0, :, c:c + LANES].astype(F32)
            if second == 1:
                dst[t, pl.ds(r, n, stride=dil), :] = rows
            else:
                mid[t, pl.ds(rl * part + rh, n, stride=second), :] = rows
        if second > 1:
            for rl in range(first):
                dst[t, pl.ds(rl, part, stride=first), :] = mid[t, rl * part:(rl + 1) * part, :]
    return jnp.concatenate([dst[t] for t in range(tiles)], axis=-1)


def _pool_mixer(tile_in_seq, u_ref, uh_ref, wpool_ref, pscale_ref):
    tm = u_ref.shape[1]
    hist = jnp.where(tile_in_seq == 0, 0.0, uh_ref[0])
    ub = jnp.concatenate([hist, u_ref[0]], axis=0)
    s2 = ub + pltpu.roll(ub, 1, 0)
    s4 = s2 + pltpu.roll(s2, 2, 0)
    s8 = s4 + pltpu.roll(s4, 4, 0)
    s16 = s8 + pltpu.roll(s8, 8, 0)
    t = lax.broadcasted_iota(jnp.int32, (tm, LANES), 0) + tile_in_seq * tm + 1
    lane = lax.broadcasted_iota(jnp.int32, (tm, LANES), 1)
    low = lane < POOL_GC
    cnt = lambda w: jnp.minimum(t, w).astype(F32)
    tile = lambda s, c: s[POOL_HALO:, c:c + LANES]
    y_lo = jnp.where(low, tile(s2, 0) / cnt(2), tile(s4, 0) / cnt(4)) - tile(ub, 0)
    y_hi = jnp.where(low, tile(s8, LANES) / cnt(8), tile(s16, LANES) / cnt(16)) - tile(ub, LANES)
    y = jnp.concatenate([y_lo, y_hi], axis=-1).astype(BF16)
    return jnp.dot(y, wpool_ref[...], preferred_element_type=F32) * pscale_ref[...]


def _group_scales(stat_refs, stages):
    stat = [_natural_order(r, d, stage) for r, d, stage in zip(stat_refs, DILATIONS, stages)]
    lane = lax.broadcasted_iota(jnp.int32, stat[0].shape, 1)
    is_max = (lane % HEAD_DIM) < STAT_SPLIT
    mx = jnp.maximum(jnp.maximum(stat[0], stat[1]), stat[2])
    e_on_max = [jnp.exp(s - mx) for s in stat]
    e_on_sum = [pltpu.roll(e, STAT_SPLIT, 1) for e in e_on_max]
    inv_on_sum = 1.0 / (stat[0] * e_on_sum[0] + stat[1] * e_on_sum[1] + stat[2] * e_on_sum[2])
    inv_on_max = pltpu.roll(inv_on_sum, GROUP_WIDTH - STAT_SPLIT, 1)
    return [jnp.where(is_max, a * inv_on_max, b * inv_on_sum) for a, b in zip(e_on_max, e_on_sum)]


def _mix_kernel(h_ref, u_ref, uh_ref, o1_ref, o4_ref, o16_ref, l1_ref, l4_ref, l16_ref, p_ref,
                wpool_ref, pscale_ref, wout_ref, g2_ref, wup_ref, wdown_ref, g3_ref, wgate_ref, wple_ref,
                gf_ref, out_ref, carry, *stages, final, tiles_per_seq, n_tiles):
    s = pl.program_id(0)
    D = h_ref.shape[2]
    tile_in_seq = jnp.minimum(s, n_tiles - 1) % tiles_per_seq
    o_refs, stat_refs = (o1_ref, o4_ref, o16_ref), (l1_ref, l4_ref, l16_ref)
    stage = iter(stages)
    take = lambda d: [next(stage) for _ in range(_stage_buffers(d))] + [None] * (2 - _stage_buffers(d))
    o_stages = [take(d) for d in DILATIONS]
    stat_stages = [take(d) for d in DILATIONS]

    def next_products():
        def product(val, row):
            return jnp.dot(val.astype(BF16), wout_ref[row:row + GROUP_WIDTH, :], preferred_element_type=F32)
        pool = product(_pool_mixer(tile_in_seq, u_ref, uh_ref, wpool_ref, pscale_ref), 0)
        scales = _group_scales(stat_refs, stat_stages)
        yield pool
        for g, dil in enumerate(DILATIONS):
            val = _natural_order(o_refs[g], dil, o_stages[g]) * scales[g]
            yield product(val, POOL_WIDTH + g * GROUP_WIDTH)

    @pl.when(s == 0)
    def _():
        carry[...] = sum(next_products())

    @pl.when(s > 0)
    def _():
        h = h_ref[0] + carry[...]
        ple = jnp.dot(p_ref[0, 0].astype(BF16), wple_ref[...], preferred_element_type=F32)
        hn = _rms(h, g2_ref[...]).astype(BF16)
        acc, nxt, products = h, 0.0, next_products()
        for c in range(0, D_FF, FF_CHUNK):
            up = jnp.dot(hn, wup_ref[:, c:c + FF_CHUNK], preferred_element_type=F32)
            act = jnp.square(jnp.maximum(up, 0.0)).astype(BF16)
            acc = acc + jnp.dot(act, wdown_ref[c:c + FF_CHUNK, :], preferred_element_type=F32)
            nxt = nxt + next(products)
        h = acc
        carry[...] = nxt

        hn = _rms(h, g3_ref[...]).astype(BF16)
        half_ple = 0.5 * ple
        base = h + half_ple
        pieces = []
        for c in range(0, D, GATE_CHUNK):
            half_z = jnp.dot(hn, wgate_ref[:, c:c + GATE_CHUNK], preferred_element_type=F32)
            pieces.append(base[:, c:c + GATE_CHUNK] + half_ple[:, c:c + GATE_CHUNK] * jnp.tanh(half_z))
        if final:
            ssq = sum(jnp.sum(x * x, axis=-1, keepdims=True) for x in pieces)
            inv = lax.rsqrt(ssq / D + EPS)
            pieces = [x * inv * gf_ref[:, c:c + GATE_CHUNK] for x, c in zip(pieces, range(0, D, GATE_CHUNK))]
        for x, c in zip(pieces, range(0, D, GATE_CHUNK)):
            out_ref[0, :, c:c + GATE_CHUNK] = x


def _mix(h, u, o, stats, p, layer, wpool, pscale, wout, g2, wup, wdown, g3, wgate, wple, gf, final):
    B, S, D = h.shape
    tm = ROW_TILE
    nt = S // tm
    n_tiles = B * nt
    ahead = lambda s: jnp.minimum(s, n_tiles - 1)
    behind = lambda s: jnp.maximum(s - 1, 0)
    row_b = lambda width: pl.BlockSpec((1, tm, width), lambda s: (behind(s) // nt, behind(s) % nt, 0))
    row_a = lambda width: pl.BlockSpec((1, tm, width), lambda s: (ahead(s) // nt, ahead(s) % nt, 0))
    halo = pl.BlockSpec(
        (1, POOL_HALO, POOL_WIDTH),
        lambda s: (ahead(s) // nt, jnp.maximum(ahead(s) % nt * (tm // POOL_HALO) - 1, 0), 0))
    strided = [pl.BlockSpec((1, tm // dil, dil * GROUP_WIDTH), lambda s: (ahead(s) // nt, ahead(s) % nt, 0))
               for dil in DILATIONS]
    p_spec = pl.BlockSpec((1, 1, tm, p.shape[-1]), lambda s: (layer, behind(s) // nt, behind(s) % nt, 0))
    weights = [wpool, pscale, wout, g2, wup, wdown, g3, wgate, wple, gf]
    return pl.pallas_call(
        functools.partial(_mix_kernel, final=final, tiles_per_seq=nt, n_tiles=n_tiles),
        grid=(n_tiles + 1,),
        in_specs=[row_b(D), row_a(POOL_WIDTH), halo] + strided * 2 + [p_spec]
                 + [_const_spec(w.shape[1:], layer) if w.ndim == 3 else _const_spec(w.shape) for w in weights],
        out_specs=row_b(D),
        out_shape=jax.ShapeDtypeStruct((B, S, D), F32),
        scratch_shapes=[pltpu.VMEM((tm, D), F32)]
                       + [pltpu.VMEM((GROUP_WIDTH // LANES, tm, LANES), F32)
                          for dil in DILATIONS * 2 for _ in range(_stage_buffers(dil))],
        compiler_params=pltpu.CompilerParams(
            dimension_semantics=("arbitrary",), vmem_limit_bytes=VMEM_LIMIT),
        name="mix_final" if final else "mix",
    )(h, u, u, *o, *stats, p, *weights)


def _rotary_parts(positions):
    inv_freq = ROPE_THETA ** (-jnp.arange(0, ROT_DIM, 2, dtype=F32) / ROT_DIM)
    ang = positions.astype(F32)[..., None] * inv_freq
    cs = jnp.concatenate([jnp.cos(ang), jnp.sin(ang)], axis=-1)
    hi = cs.astype(BF16)
    rest = cs - hi.astype(F32)
    mid = rest.astype(BF16)
    lo = (rest - mid.astype(F32)).astype(BF16)
    lane = np.arange(LANES)
    rotary = (lane % HEAD_DIM) < ROT_DIM
    pick = ((lane[None, :] % _HALF == np.arange(_HALF)[:, None]) & rotary[None, :]).astype(np.float32)
    sign = np.where(lane < HEAD_DIM, -1.0, 1.0).astype(np.float32)
    zero = np.zeros_like(pick)
    one_part = np.block([[pick, zero], [zero, pick * sign]])
    sel = np.concatenate([one_part] * 3, axis=0)
    return jnp.concatenate([hi, mid, lo], axis=-1), jnp.asarray(sel, BF16)


def _block_diag(w):
    g, c, _ = w.shape
    eye = jnp.eye(g, dtype=w.dtype)
    return (w[:, :, None, :] * eye[:, None, :, None]).reshape(g * c, g * c)


def kernel(x, p, positions, norm1, w_in, pool_w, pool_scale, w_out, norm2, w_up, w_down, norm3, w_gate,
           w_ple, final_norm):
    depth = w_in.shape[0]
    n_grp = len(DILATIONS)
    cs, sel = _rotary_parts(positions)
    bias = _band_bias()
    vec = lambda a: a.reshape(1, -1)
    w_in_b, w_out_b, w_up_b, w_down_b, w_gate_b, w_ple_b = _prepare_weights(
        w_in, (w_out, w_up, w_down, w_gate, w_ple), (1.0, 1.0, 1.0, 0.5, 1.0))
    h = x
    for i in range(depth):
        u, *qkv = _inproj(h, vec(norm1[i]), w_in_b, i, cs, sel)
        o, stats = [], []
        for g, dil in enumerate(DILATIONS):
            og, sg = _attention(qkv[g], qkv[n_grp + g], qkv[2 * n_grp + g], bias, dil)
            o.append(og)
            stats.append(sg)
        h = _mix(h, u, o, stats, p, i, _block_diag(pool_w[i]).astype(BF16), vec(pool_scale[i]),
                 w_out_b, vec(norm2[i]), w_up_b, w_down_b, vec(norm3[i]), w_gate_b, w_ple_b, vec(final_norm),
                 final=(i == depth - 1))
    return h
```

```python
import functools

import numpy as np
import jax
import jax.numpy as jnp
from jax import lax
from jax.experimental import pallas as pl
from jax.experimental.pallas import tpu as pltpu

F32 = jnp.float32
BF16 = jnp.bfloat16

D_MODEL = 1024
HEAD_DIM = 64
POOL_WIDTH = 256
POOL_WINDOWS = (2, 4, 8, 16)
POOL_GC = POOL_WIDTH // len(POOL_WINDOWS)
ATTN_WIDTH = D_MODEL - POOL_WIDTH
DILATIONS = (1, 4, 16)
GROUP_WIDTH = ATTN_WIDTH // len(DILATIONS)
ROT_DIM = HEAD_DIM // 4
ROPE_THETA = 500000.0
BLK = 128
D_FF = 4 * D_MODEL
EPS = 1e-6
MASK_VALUE = -1e30

LANES = 128
POOL_HALO = 16
ROW_TILE = 512
INPROJ_TILE = 1024
ATTN_ROWS = 4096
ATTN_MIN_ROWS = 512
ATTN_UNITS = 64
STAT_SPLIT = HEAD_DIM // 2
FF_CHUNK = 1024
GATE_CHUNK = 256
VMEM_LIMIT = 56 * 1024 * 1024
WEIGHT_SLABS = 8
SUBLANE_STRIDE = 4


def _rms(x, g):
    return x * lax.rsqrt(jnp.mean(x * x, axis=-1, keepdims=True) + EPS) * g


def _const_spec(shape, layer=None):
    zeros = (0,) * len(shape)
    if layer is None:
        return pl.BlockSpec(shape, lambda *_: zeros, pipeline_mode=pl.Buffered(1))
    return pl.BlockSpec((None,) + tuple(shape), lambda *_: (layer,) + zeros, pipeline_mode=pl.Buffered(1))


_HALF = ROT_DIM // 2
_QK_TILE_PERM = np.concatenate([
    np.arange(0, _HALF), np.arange(HEAD_DIM, HEAD_DIM + _HALF), np.arange(ROT_DIM, HEAD_DIM),
    np.arange(_HALF, ROT_DIM), np.arange(HEAD_DIM + _HALF, HEAD_DIM + ROT_DIM),
    np.arange(HEAD_DIM + ROT_DIM, LANES)])
_QKV_ORDER = ((0, 2), (0, 0), (1, 2), (0, 1), (2, 2), (1, 0), (1, 1), (2, 0), (2, 1))


def _residue_rows(src, mid, dil):
    tm = src.shape[0]
    if dil <= SUBLANE_STRIDE:
        for r in range(dil):
            yield r, src[pl.ds(r, tm // dil, stride=dil), :]
        return
    first, second = SUBLANE_STRIDE, dil // SUBLANE_STRIDE
    part = tm // first
    for rl in range(first):
        mid[rl * part:(rl + 1) * part, :] = src[pl.ds(rl, part, stride=first), :]
    for r in range(dil):
        rh, rl = divmod(r, first)
        yield r, mid[pl.ds(rl * part + rh, tm // dil, stride=second), :]


def _prep_kernel(*refs, scales):
    n = len(refs) // 2
    (win_ref, *plain_in), (win_out, *plain_out) = refs[:n], refs[n:]
    for w_ref, o_ref, scale in zip(plain_in, plain_out, scales):
        w = w_ref[0]
        o_ref[0] = (w if scale == 1.0 else w * scale).astype(o_ref.dtype)
    lane = lax.broadcasted_iota(jnp.int32, (win_ref.shape[1], LANES), 1)
    swap = HEAD_DIM - _HALF
    from_high = (lane >= _HALF) & (lane < ROT_DIM)
    from_low = (lane >= HEAD_DIM) & (lane < HEAD_DIM + _HALF)
    for c in range(0, win_ref.shape[2], LANES):
        w = win_ref[0, :, c:c + LANES]
        if POOL_WIDTH <= c < POOL_WIDTH + 2 * ATTN_WIDTH:
            w = jnp.where(from_high, pltpu.roll(w, LANES - swap, 1), jnp.where(from_low, pltpu.roll(w, swap, 1), w))
        if POOL_WIDTH <= c < POOL_WIDTH + ATTN_WIDTH:
            w = w * HEAD_DIM ** -0.5
        win_out[0, :, c:c + LANES] = w.astype(win_out.dtype)


def _prepare_weights(w_in, plain, scales):
    weights = (w_in,) + tuple(plain)
    depth = w_in.shape[0]
    slab = lambda w: pl.BlockSpec((1, w.shape[1] // WEIGHT_SLABS, w.shape[2]), lambda l, i: (l, i, 0))
    return pl.pallas_call(
        functools.partial(_prep_kernel, scales=tuple(scales)),
        grid=(depth, WEIGHT_SLABS),
        in_specs=[slab(w) for w in weights],
        out_specs=[slab(w) for w in weights],
        out_shape=[jax.ShapeDtypeStruct(w.shape, BF16) for w in weights],
        compiler_params=pltpu.CompilerParams(
            dimension_semantics=("parallel", "parallel"), vmem_limit_bytes=VMEM_LIMIT),
        name="weight_prep",
    )(*weights)


def _head_a_lanes(lane):
    return (lane < _HALF) | ((lane >= ROT_DIM) & (lane < HEAD_DIM + _HALF))


def _inproj_kernel(x_ref, g_ref, w_ref, cs_ref, sel_ref, u_ref, *refs):
    n_grp = len(DILATIONS)
    qkv_refs, (hnbuf, zbuf, *stages) = refs[:3 * n_grp], refs[3 * n_grp:]
    tm = x_ref.shape[1]
    s = pl.program_id(0)

    def column(kind, g):
        return POOL_WIDTH + kind * ATTN_WIDTH + g * GROUP_WIDTH

    def prepare_next():
        hn_next = _rms(x_ref[0], g_ref[...]).astype(BF16)
        first = column(*_QKV_ORDER[0])
        zbuf[...] = jnp.dot(hn_next, w_ref[:, first:first + GROUP_WIDTH], preferred_element_type=F32)
        hnbuf[...] = hn_next

    @pl.when(s == 0)
    def _():
        prepare_next()

    @pl.when(s > 0)
    def _():
        table = jnp.dot(cs_ref[0], sel_ref[...], preferred_element_type=F32)
        lane = lax.broadcasted_iota(jnp.int32, (1, LANES), 1)
        cos_t = table[:, :LANES] + ((lane % HEAD_DIM) >= ROT_DIM).astype(F32)
        sin_t = table[:, LANES:]
        hn = hnbuf[...]

        for b, (kind, g) in enumerate(_QKV_ORDER):
            out_ref = qkv_refs[kind * n_grp + g]
            if b == 0:
                z = zbuf[...]
            else:
                col = column(kind, g)
                z = jnp.dot(hn, w_ref[:, col:col + GROUP_WIDTH], preferred_element_type=F32)
            halves = [z[:, c:c + LANES] for c in range(0, GROUP_WIDTH, LANES)]
            if kind < 2:
                halves = [z_h * cos_t + pltpu.roll(z_h, LANES // 2, 1) * sin_t for z_h in halves]
            dil = DILATIONS[g]
            if dil == 1:
                out_ref[0] = jnp.concatenate(halves, axis=-1).astype(out_ref.dtype)
                continue
            src, mid = stages[2 * (b % 2)], stages[2 * (b % 2) + 1]
            for t, z_h in enumerate(halves):
                src[t] = z_h
                for r, rows in _residue_rows(src.at[t], mid.at[t], dil):
                    c = r * GROUP_WIDTH + t * LANES
                    out_ref[0, :, c:c + LANES] = rows.astype(out_ref.dtype)
        u_ref[0] = jnp.dot(hn, w_ref[:, :POOL_WIDTH], preferred_element_type=F32)
        prepare_next()


def _inproj(x, g, w_in, layer, cs, sel):
    B, S, D = x.shape
    n_in = w_in.shape[2]
    tm = INPROJ_TILE
    nt = S // tm
    n_tiles = B * nt
    ahead = lambda s: jnp.minimum(s, n_tiles - 1)
    behind = lambda s: jnp.maximum(s - 1, 0)
    row = lambda width, tile: pl.BlockSpec((1, tm, width), lambda s: (tile(s) // nt, tile(s) % nt, 0))
    strided = lambda dil: pl.BlockSpec((1, tm // dil, dil * GROUP_WIDTH),
                                       lambda s: (behind(s) // nt, behind(s) % nt, 0))
    out_shape = [jax.ShapeDtypeStruct((B, S, POOL_WIDTH), F32)]
    out_shape += [jax.ShapeDtypeStruct((B, S // dil, dil * GROUP_WIDTH), BF16) for dil in DILATIONS] * 3
    return pl.pallas_call(
        _inproj_kernel,
        grid=(n_tiles + 1,),
        in_specs=[row(D, ahead), _const_spec((1, D)), _const_spec((D, n_in), layer), row(cs.shape[-1], behind),
                  _const_spec(sel.shape)],
        out_specs=[row(POOL_WIDTH, behind)] + [strided(dil) for dil in DILATIONS] * 3,
        out_shape=out_shape,
        scratch_shapes=[pltpu.VMEM((tm, D), BF16), pltpu.VMEM((tm, GROUP_WIDTH), F32)]
                       + [pltpu.VMEM((GROUP_WIDTH // LANES, tm, LANES), F32)] * 4,
        compiler_params=pltpu.CompilerParams(
            dimension_semantics=("arbitrary",), vmem_limit_bytes=VMEM_LIMIT),
        name="inproj",
    )(x, g, w_in, cs, sel)


def _attn_kernel(q_ref, k_ref, kh_ref, v_ref, vh_ref, bias_ref, o_ref, stat_ref, kbuf, vbuf):
    rows, cols = q_ref.shape[1], q_ref.shape[2]
    n_blk = rows // BLK
    first_chunk = pl.program_id(1) == 0

    kbuf[0:BLK] = kh_ref[0]
    kbuf[BLK:] = k_ref[0]
    vbuf[0:BLK] = vh_ref[0]
    vbuf[BLK:] = v_ref[0]

    lane = lax.broadcasted_iota(jnp.int32, (BLK, LANES), 1)
    head0 = lane < HEAD_DIM
    head0_qk = _head_a_lanes(lane)
    stat_is_max = (lane % HEAD_DIM) < STAT_SPLIT
    ones = jnp.ones((2 * BLK, LANES), BF16)

    def unit(i, c):
        r0 = pl.multiple_of(i * BLK, BLK)
        q = q_ref[0, pl.ds(r0, BLK), c:c + LANES]
        kwin = kbuf[pl.ds(r0, 2 * BLK), c:c + LANES]
        vwin = vbuf[pl.ds(r0, 2 * BLK), c:c + LANES]
        zero = jnp.zeros_like(q)
        q2 = jnp.concatenate([jnp.where(head0_qk, q, zero), jnp.where(head0_qk, zero, q)], axis=0)
        s = lax.dot_general(q2, kwin, (((1,), (1,)), ((), ())), preferred_element_type=F32)
        no_history = jnp.logical_and(first_chunk, i == 0).astype(jnp.int32)
        s = s + bias_ref[no_history]
        m = jnp.max(s, axis=-1, keepdims=True)
        e = jnp.exp(s - m).astype(BF16)
        pv = jnp.dot(e, jnp.concatenate([vwin, ones], axis=1), preferred_element_type=F32)
        o_ref[0, pl.ds(r0, BLK), c:c + LANES] = jnp.where(
            head0, pv[:BLK, :LANES], pv[BLK:, :LANES]).astype(o_ref.dtype)
        m_both = jnp.where(head0, jnp.broadcast_to(m[:BLK], (BLK, LANES)), jnp.broadcast_to(m[BLK:], (BLK, LANES)))
        l_both = jnp.where(head0, pv[:BLK, LANES:], pv[BLK:, LANES:])
        stat_ref[0, pl.ds(r0, BLK), c:c + LANES] = jnp.where(stat_is_max, m_both, l_both)

    tiles = cols // LANES
    tiles_per_body = min(tiles, ATTN_UNITS)
    blocks_per_body = ATTN_UNITS // tiles_per_body
    for c0 in range(0, tiles, tiles_per_body):
        def body(ib, carry, c0=c0):
            for bi in range(blocks_per_body):
                for ci in range(tiles_per_body):
                    unit(ib * blocks_per_body + bi, (c0 + ci) * LANES)
            return carry
        lax.fori_loop(0, n_blk // blocks_per_body, body, 0)


def _attention(q, k, v, bias, dil):
    B, L, width = q.shape
    W = width // dil
    rows = min(L, max(ATTN_ROWS // dil, ATTN_MIN_ROWS))
    cols = (ATTN_ROWS // rows) * W
    hist_blocks = rows // BLK
    main = pl.BlockSpec((1, rows, cols), lambda b, n, r: (b, n, r))
    hist = pl.BlockSpec((1, BLK, cols), lambda b, n, r: (b, jnp.maximum(n * hist_blocks - 1, 0), r))
    return pl.pallas_call(
        _attn_kernel,
        grid=(B, L // rows, dil * W // cols),
        in_specs=[main, main, hist, main, hist, _const_spec(bias.shape)],
        out_specs=[main, main],
        out_shape=[jax.ShapeDtypeStruct((B, L, dil * W), BF16), jax.ShapeDtypeStruct((B, L, dil * W), F32)],
        scratch_shapes=[pltpu.VMEM((BLK + rows, cols), BF16), pltpu.VMEM((BLK + rows, cols), BF16)],
        compiler_params=pltpu.CompilerParams(
            dimension_semantics=("parallel", "arbitrary", "parallel"), vmem_limit_bytes=VMEM_LIMIT),
        name=f"attn_d{dil}",
    )(q, k, k, v, v, bias)


def _band_bias():
    a = np.arange(2 * BLK)[:, None] % BLK
    c = np.arange(2 * BLK)[None, :]
    band = (c >= a) & (c <= a + BLK)
    both = np.stack([band, band & (c >= BLK)])
    return jnp.asarray(np.where(both, 0.0, MASK_VALUE), F32)


def _stage_buffers(dil):
    return 0 if dil == 1 else 1 if dil <= SUBLANE_STRIDE else 2


def _natural_order(ref, dil, stage):
    if dil == 1:
        return ref[0].astype(F32)
    n = ref.shape[1]
    tm = n * dil
    tiles = GROUP_WIDTH // LANES
    dst, mid = stage
    first = min(dil, SUBLANE_STRIDE)
    second, part = dil // first, tm // first
    for t in range(tiles):
        for r in range(dil):
            rh, rl = divmod(r, first)
            c = r * GROUP_WIDTH + t * LANES
            rows = ref[0, :, c:c + LANES].astype(F32)
            if second == 1:
                dst[t, pl.ds(r, n, stride=dil), :] = rows
            else:
                mid[t, pl.ds(rl * part + rh, n, stride=second), :] = rows
        if second > 1:
            for rl in range(first):
                dst[t, pl.ds(rl, part, stride=first), :] = mid[t, rl * part:(rl + 1) * part, :]
    return jnp.concatenate([dst[t] for t in range(tiles)], axis=-1)


def _pool_mixer(tile_in_seq, u_ref, uh_ref, wpool_ref, pscale_ref):
    tm = u_ref.shape[1]
    hist = jnp.where(tile_in_seq == 0, 0.0, uh_ref[0])
    ub = jnp.concatenate([hist, u_ref[0]], axis=0)
    s2 = ub + pltpu.roll(ub, 1, 0)
    s4 = s2 + pltpu.roll(s2, 2, 0)
    s8 = s4 + pltpu.roll(s4, 4, 0)
    s16 = s8 + pltpu.roll(s8, 8, 0)
    t = lax.broadcasted_iota(jnp.int32, (tm, LANES), 0) + tile_in_seq * tm + 1
    lane = lax.broadcasted_iota(jnp.int32, (tm, LANES), 1)
    low = lane < POOL_GC
    cnt = lambda w: jnp.minimum(t, w).astype(F32)
    tile = lambda s, c: s[POOL_HALO:, c:c + LANES]
    y_lo = jnp.where(low, tile(s2, 0) / cnt(2), tile(s4, 0) / cnt(4)) - tile(ub, 0)
    y_hi = jnp.where(low, tile(s8, LANES) / cnt(8), tile(s16, LANES) / cnt(16)) - tile(ub, LANES)
    y = jnp.concatenate([y_lo, y_hi], axis=-1).astype(BF16)
    return jnp.dot(y, wpool_ref[...], preferred_element_type=F32) * pscale_ref[...]


def _group_scales(stat_refs, stages):
    stat = [_natural_order(r, d, stage) for r, d, stage in zip(stat_refs, DILATIONS, stages)]
    lane = lax.broadcasted_iota(jnp.int32, stat[0].shape, 1)
    is_max = (lane % HEAD_DIM) < STAT_SPLIT
    mx = jnp.maximum(jnp.maximum(stat[0], stat[1]), stat[2])
    e_on_max = [jnp.exp(s - mx) for s in stat]
    e_on_sum = [pltpu.roll(e, STAT_SPLIT, 1) for e in e_on_max]
    inv_on_sum = 1.0 / (stat[0] * e_on_sum[0] + stat[1] * e_on_sum[1] + stat[2] * e_on_sum[2])
    inv_on_max = pltpu.roll(inv_on_sum, GROUP_WIDTH - STAT_SPLIT, 1)
    return [jnp.where(is_max, a * inv_on_max, b * inv_on_sum) for a, b in zip(e_on_max, e_on_sum)]


def _mix_kernel(h_ref, u_ref, uh_ref, o1_ref, o4_ref, o16_ref, l1_ref, l4_ref, l16_ref, p_ref,
                wpool_ref, pscale_ref, wout_ref, g2_ref, wup_ref, wdown_ref, g3_ref, wgate_ref, wple_ref,
                gf_ref, out_ref, carry, *stages, final, tiles_per_seq, n_tiles):
    s = pl.program_id(0)
    D = h_ref.shape[2]
    tile_in_seq = jnp.minimum(s, n_tiles - 1) % tiles_per_seq
    o_refs, stat_refs = (o1_ref, o4_ref, o16_ref), (l1_ref, l4_ref, l16_ref)
    stage = iter(stages)
    take = lambda d: [next(stage) for _ in range(_stage_buffers(d))] + [None] * (2 - _stage_buffers(d))
    o_stages = [take(d) for d in DILATIONS]
    stat_stages = [take(d) for d in DILATIONS]

    def next_products():
        def product(val, row):
            return jnp.dot(val.astype(BF16), wout_ref[row:row + GROUP_WIDTH, :], preferred_element_type=F32)
        pool = product(_pool_mixer(tile_in_seq, u_ref, uh_ref, wpool_ref, pscale_ref), 0)
        scales = _group_scales(stat_refs, stat_stages)
        yield pool
        for g, dil in enumerate(DILATIONS):
            val = _natural_order(o_refs[g], dil, o_stages[g]) * scales[g]
            yield product(val, POOL_WIDTH + g * GROUP_WIDTH)

    @pl.when(s == 0)
    def _():
        carry[...] = sum(next_products())

    @pl.when(s > 0)
    def _():
        h = h_ref[0] + carry[...]
        ple = jnp.dot(p_ref[0, 0].astype(BF16), wple_ref[...], preferred_element_type=F32)
        hn = _rms(h, g2_ref[...]).astype(BF16)
        acc, nxt, products = h, 0.0, next_products()
        for c in range(0, D_FF, FF_CHUNK):
            up = jnp.dot(hn, wup_ref[:, c:c + FF_CHUNK], preferred_element_type=F32)
            act = jnp.square(jnp.maximum(up, 0.0)).astype(BF16)
            acc = acc + jnp.dot(act, wdown_ref[c:c + FF_CHUNK, :], preferred_element_type=F32)
            nxt = nxt + next(products)
        h = acc
        carry[...] = nxt

        hn = _rms(h, g3_ref[...]).astype(BF16)
        half_ple = 0.5 * ple
        base = h + half_ple
        pieces = []
        for c in range(0, D, GATE_CHUNK):
            half_z = jnp.dot(hn, wgate_ref[:, c:c + GATE_CHUNK], preferred_element_type=F32)
            pieces.append(base[:, c:c + GATE_CHUNK] + half_ple[:, c:c + GATE_CHUNK] * jnp.tanh(half_z))
        if final:
            ssq = sum(jnp.sum(x * x, axis=-1, keepdims=True) for x in pieces)
            inv = lax.rsqrt(ssq / D + EPS)
            pieces = [x * inv * gf_ref[:, c:c + GATE_CHUNK] for x, c in zip(pieces, range(0, D, GATE_CHUNK))]
        for x, c in zip(pieces, range(0, D, GATE_CHUNK)):
            out_ref[0, :, c:c + GATE_CHUNK] = x


def _mix(h, u, o, stats, p, layer, wpool, pscale, wout, g2, wup, wdown, g3, wgate, wple, gf, final):
    B, S, D = h.shape
    tm = ROW_TILE
    nt = S // tm
    n_tiles = B * nt
    ahead = lambda s: jnp.minimum(s, n_tiles - 1)
    behind = lambda s: jnp.maximum(s - 1, 0)
    row_b = lambda width: pl.BlockSpec((1, tm, width), lambda s: (behind(s) // nt, behind(s) % nt, 0))
    row_a = lambda width: pl.BlockSpec((1, tm, width), lambda s: (ahead(s) // nt, ahead(s) % nt, 0))
    halo = pl.BlockSpec(
        (1, POOL_HALO, POOL_WIDTH),
        lambda s: (ahead(s) // nt, jnp.maximum(ahead(s) % nt * (tm // POOL_HALO) - 1, 0), 0))
    strided = [pl.BlockSpec((1, tm // dil, dil * GROUP_WIDTH), lambda s: (ahead(s) // nt, ahead(s) % nt, 0))
               for dil in DILATIONS]
    p_spec = pl.BlockSpec((1, 1, tm, p.shape[-1]), lambda s: (layer, behind(s) // nt, behind(s) % nt, 0))
    weights = [wpool, pscale, wout, g2, wup, wdown, g3, wgate, wple, gf]
    return pl.pallas_call(
        functools.partial(_mix_kernel, final=final, tiles_per_seq=nt, n_tiles=n_tiles),
        grid=(n_tiles + 1,),
        in_specs=[row_b(D), row_a(POOL_WIDTH), halo] + strided * 2 + [p_spec]
                 + [_const_spec(w.shape[1:], layer) if w.ndim == 3 else _const_spec(w.shape) for w in weights],
        out_specs=row_b(D),
        out_shape=jax.ShapeDtypeStruct((B, S, D), F32),
        scratch_shapes=[pltpu.VMEM((tm, D), F32)]
                       + [pltpu.VMEM((GROUP_WIDTH // LANES, tm, LANES), F32)
                          for dil in DILATIONS * 2 for _ in range(_stage_buffers(dil))],
        compiler_params=pltpu.CompilerParams(
            dimension_semantics=("arbitrary",), vmem_limit_bytes=VMEM_LIMIT),
        name="mix_final" if final else "mix",
    )(h, u, u, *o, *stats, p, *weights)


def _rotary_parts(positions):
    inv_freq = ROPE_THETA ** (-jnp.arange(0, ROT_DIM, 2, dtype=F32) / ROT_DIM)
    ang = positions.astype(F32)[..., None] * inv_freq
    cs = jnp.concatenate([jnp.cos(ang), jnp.sin(ang)], axis=-1)
    hi = cs.astype(BF16)
    rest = cs - hi.astype(F32)
    mid = rest.astype(BF16)
    lo = (rest - mid.astype(F32)).astype(BF16)
    lane = np.arange(LANES)
    rotary = (lane % HEAD_DIM) < ROT_DIM
    pick = ((lane[None, :] % _HALF == np.arange(_HALF)[:, None]) & rotary[None, :]).astype(np.float32)
    sign = np.where(lane < HEAD_DIM, -1.0, 1.0).astype(np.float32)
    zero = np.zeros_like(pick)
    one_part = np.block([[pick, zero], [zero, pick * sign]])
    sel = np.concatenate([one_part] * 3, axis=0)
    return jnp.concatenate([hi, mid, lo], axis=-1), jnp.asarray(sel, BF16)


def _block_diag(w):
    g, c, _ = w.shape
    eye = jnp.eye(g, dtype=w.dtype)
    return (w[:, :, None, :] * eye[:, None, :, None]).reshape(g * c, g * c)


def kernel(x, p, positions, norm1, w_in, pool_w, pool_scale, w_out, norm2, w_up, w_down, norm3, w_gate,
           w_ple, final_norm):
    depth = w_in.shape[0]
    n_grp = len(DILATIONS)
    cs, sel = _rotary_parts(positions)
    bias = _band_bias()
    vec = lambda a: a.reshape(1, -1)
    w_in_b, w_out_b, w_up_b, w_down_b, w_gate_b, w_ple_b = _prepare_weights(
        w_in, (w_out, w_up, w_down, w_gate, w_ple), (1.0, 1.0, 1.0, 0.5, 1.0))
    h = x
    for i in range(depth):
        u, *qkv = _inproj(h, vec(norm1[i]), w_in_b, i, cs, sel)
        o, stats = [], []
        for g, dil in enumerate(DILATIONS):
            og, sg = _attention(qkv[g], qkv[n_grp + g], qkv[2 * n_grp + g], bias, dil)
            o.append(og)
            stats.append(sg)
        h = _mix(h, u, o, stats, p, i, _block_diag(pool_w[i]).astype(BF16), vec(pool_scale[i]),
                 w_out_b, vec(norm2[i]), w_up_b, w_down_b, vec(norm3[i]), w_gate_b, w_ple_b, vec(final_norm),
                 final=(i == depth - 1))
    return h
```

```python
import functools

import numpy as np
import jax
import jax.numpy as jnp
from jax import lax
from jax.experimental import pallas as pl
from jax.experimental.pallas import tpu as pltpu

F32 = jnp.float32
BF16 = jnp.bfloat16

D_MODEL = 1024
HEAD_DIM = 64
POOL_WIDTH = 256
POOL_WINDOWS = (2, 4, 8, 16)
POOL_GC = POOL_WIDTH // len(POOL_WINDOWS)
ATTN_WIDTH = D_MODEL - POOL_WIDTH
DILATIONS = (1, 4, 16)
GROUP_WIDTH = ATTN_WIDTH // len(DILATIONS)
ROT_DIM = HEAD_DIM // 4
ROPE_THETA = 500000.0
BLK = 128
D_FF = 4 * D_MODEL
EPS = 1e-6
MASK_VALUE = -1e30

LANES = 128
POOL_HALO = 16
ROW_TILE = 512
INPROJ_TILE = 1024
ATTN_ROWS = 4096
ATTN_MIN_ROWS = 512
ATTN_UNITS = 64
STAT_SPLIT = HEAD_DIM // 2
FF_CHUNK = 1024
GATE_CHUNK = 256
VMEM_LIMIT = 56 * 1024 * 1024
WEIGHT_SLABS = 8
SUBLANE_STRIDE = 4


def _rms(x, g):
    return x * lax.rsqrt(jnp.mean(x * x, axis=-1, keepdims=True) + EPS) * g


def _const_spec(shape, layer=None):
    zeros = (0,) * len(shape)
    if layer is None:
        return pl.BlockSpec(shape, lambda *_: zeros, pipeline_mode=pl.Buffered(1))
    return pl.BlockSpec((None,) + tuple(shape), lambda *_: (layer,) + zeros, pipeline_mode=pl.Buffered(1))


_HALF = ROT_DIM // 2
_QK_TILE_PERM = np.concatenate([
    np.arange(0, _HALF), np.arange(HEAD_DIM, HEAD_DIM + _HALF), np.arange(ROT_DIM, HEAD_DIM),
    np.arange(_HALF, ROT_DIM), np.arange(HEAD_DIM + _HALF, HEAD_DIM + ROT_DIM),
    np.arange(HEAD_DIM + ROT_DIM, LANES)])
_QKV_ORDER = ((0, 2), (0, 0), (1, 2), (0, 1), (2, 2), (1, 0), (1, 1), (2, 0), (2, 1))


def _residue_rows(src, mid, dil):
    tm = src.shape[0]
    if dil <= SUBLANE_STRIDE:
        for r in range(dil):
            yield r, src[pl.ds(r, tm // dil, stride=dil), :]
        return
    first, second = SUBLANE_STRIDE, dil // SUBLANE_STRIDE
    part = tm // first
    for rl in range(first):
        mid[rl * part:(rl + 1) * part, :] = src[pl.ds(rl, part, stride=first), :]
    for r in range(dil):
        rh, rl = divmod(r, first)
        yield r, mid[pl.ds(rl * part + rh, tm // dil, stride=second), :]


def _prep_kernel(*refs, scales):
    n = len(refs) // 2
    (win_ref, *plain_in), (win_out, *plain_out) = refs[:n], refs[n:]
    for w_ref, o_ref, scale in zip(plain_in, plain_out, scales):
        w = w_ref[0]
        o_ref[0] = (w if scale == 1.0 else w * scale).astype(o_ref.dtype)
    lane = lax.broadcasted_iota(jnp.int32, (win_ref.shape[1], LANES), 1)
    swap = HEAD_DIM - _HALF
    from_high = (lane >= _HALF) & (lane < ROT_DIM)
    from_low = (lane >= HEAD_DIM) & (lane < HEAD_DIM + _HALF)
    for c in range(0, win_ref.shape[2], LANES):
        w = win_ref[0, :, c:c + LANES]
        if POOL_WIDTH <= c < POOL_WIDTH + 2 * ATTN_WIDTH:
            w = jnp.where(from_high, pltpu.roll(w, LANES - swap, 1), jnp.where(from_low, pltpu.roll(w, swap, 1), w))
        if POOL_WIDTH <= c < POOL_WIDTH + ATTN_WIDTH:
            w = w * HEAD_DIM ** -0.5
        win_out[0, :, c:c + LANES] = w.astype(win_out.dtype)


def _prepare_weights(w_in, plain, scales):
    weights = (w_in,) + tuple(plain)
    depth = w_in.shape[0]
    slab = lambda w: pl.BlockSpec((1, w.shape[1] // WEIGHT_SLABS, w.shape[2]), lambda l, i: (l, i, 0))
    return pl.pallas_call(
        functools.partial(_prep_kernel, scales=tuple(scales)),
        grid=(depth, WEIGHT_SLABS),
        in_specs=[slab(w) for w in weights],
        out_specs=[slab(w) for w in weights],
        out_shape=[jax.ShapeDtypeStruct(w.shape, BF16) for w in weights],
        compiler_params=pltpu.CompilerParams(
            dimension_semantics=("parallel", "parallel"), vmem_limit_bytes=VMEM_LIMIT),
        name="weight_prep",
    )(*weights)


def _head_a_lanes(lane):
    return (lane < _HALF) | ((lane >= ROT_DIM) & (lane < HEAD_DIM + _HALF))


def _inproj_kernel(x_ref, g_ref, w_ref, cs_ref, sel_ref, u_ref, *refs):
    n_grp = len(DILATIONS)
    qkv_refs, (hnbuf, zbuf, *stages) = refs[:3 * n_grp], refs[3 * n_grp:]
    tm = x_ref.shape[1]
    s = pl.program_id(0)

    def column(kind, g):
        return POOL_WIDTH + kind * ATTN_WIDTH + g * GROUP_WIDTH

    def prepare_next():
        hn_next = _rms(x_ref[0], g_ref[...]).astype(BF16)
        first = column(*_QKV_ORDER[0])
        zbuf[...] = jnp.dot(hn_next, w_ref[:, first:first + GROUP_WIDTH], preferred_element_type=F32)
        hnbuf[...] = hn_next

    @pl.when(s == 0)
    def _():
        prepare_next()

    @pl.when(s > 0)
    def _():
        table = jnp.dot(cs_ref[0], sel_ref[...], preferred_element_type=F32)
        lane = lax.broadcasted_iota(jnp.int32, (1, LANES), 1)
        cos_t = table[:, :LANES] + ((lane % HEAD_DIM) >= ROT_DIM).astype(F32)
        sin_t = table[:, LANES:]
        hn = hnbuf[...]

        for b, (kind, g) in enumerate(_QKV_ORDER):
            out_ref = qkv_refs[kind * n_grp + g]
            if b == 0:
                z = zbuf[...]
            else:
                col = column(kind, g)
                z = jnp.dot(hn, w_ref[:, col:col + GROUP_WIDTH], preferred_element_type=F32)
            halves = [z[:, c:c + LANES] for c in range(0, GROUP_WIDTH, LANES)]
            if kind < 2:
                halves = [z_h * cos_t + pltpu.roll(z_h, LANES // 2, 1) * sin_t for z_h in halves]
            dil = DILATIONS[g]
            if dil == 1:
                out_ref[0] = jnp.concatenate(halves, axis=-1).astype(out_ref.dtype)
                continue
            src, mid = stages[2 * (b % 2)], stages[2 * (b % 2) + 1]
            for t, z_h in enumerate(halves):
                src[t] = z_h
                for r, rows in _residue_rows(src.at[t], mid.at[t], dil):
                    c = r * GROUP_WIDTH + t * LANES
                    out_ref[0, :, c:c + LANES] = rows.astype(out_ref.dtype)
        u_ref[0] = jnp.dot(hn, w_ref[:, :POOL_WIDTH], preferred_element_type=F32)
        prepare_next()


def _inproj(x, g, w_in, layer, cs, sel):
    B, S, D = x.shape
    n_in = w_in.shape[2]
    tm = INPROJ_TILE
    nt = S // tm
    n_tiles = B * nt
    ahead = lambda s: jnp.minimum(s, n_tiles - 1)
    behind = lambda s: jnp.maximum(s - 1, 0)
    row = lambda width, tile: pl.BlockSpec((1, tm, width), lambda s: (tile(s) // nt, tile(s) % nt, 0))
    strided = lambda dil: pl.BlockSpec((1, tm // dil, dil * GROUP_WIDTH),
                                       lambda s: (behind(s) // nt, behind(s) % nt, 0))
    out_shape = [jax.ShapeDtypeStruct((B, S, POOL_WIDTH), F32)]
    out_shape += [jax.ShapeDtypeStruct((B, S // dil, dil * GROUP_WIDTH), BF16) for dil in DILATIONS] * 3
    return pl.pallas_call(
        _inproj_kernel,
        grid=(n_tiles + 1,),
        in_specs=[row(D, ahead), _const_spec((1, D)), _const_spec((D, n_in), layer), row(cs.shape[-1], behind),
                  _const_spec(sel.shape)],
        out_specs=[row(POOL_WIDTH, behind)] + [strided(dil) for dil in DILATIONS] * 3,
        out_shape=out_shape,
        scratch_shapes=[pltpu.VMEM((tm, D), BF16), pltpu.VMEM((tm, GROUP_WIDTH), F32)]
                       + [pltpu.VMEM((GROUP_WIDTH // LANES, tm, LANES), F32)] * 4,
        compiler_params=pltpu.CompilerParams(
            dimension_semantics=("arbitrary",), vmem_limit_bytes=VMEM_LIMIT),
        name="inproj",
    )(x, g, w_in, cs, sel)


def _attn_kernel(q_ref, k_ref, kh_ref, v_ref, vh_ref, bias_ref, o_ref, stat_ref, kbuf, vbuf):
    rows, cols = q_ref.shape[1], q_ref.shape[2]
    n_blk = rows // BLK
    first_chunk = pl.program_id(1) == 0

    kbuf[0:BLK] = kh_ref[0]
    kbuf[BLK:] = k_ref[0]
    vbuf[0:BLK] = vh_ref[0]
    vbuf[BLK:] = v_ref[0]

    lane = lax.broadcasted_iota(jnp.int32, (BLK, LANES), 1)
    head0 = lane < HEAD_DIM
    head0_qk = _head_a_lanes(lane)
    stat_is_max = (lane % HEAD_DIM) < STAT_SPLIT
    ones = jnp.ones((2 * BLK, LANES), BF16)

    def unit(i, c):
        r0 = pl.multiple_of(i * BLK, BLK)
        q = q_ref[0, pl.ds(r0, BLK), c:c + LANES]
        kwin = kbuf[pl.ds(r0, 2 * BLK), c:c + LANES]
        vwin = vbuf[pl.ds(r0, 2 * BLK), c:c + LANES]
        zero = jnp.zeros_like(q)
        q2 = jnp.concatenate([jnp.where(head0_qk, q, zero), jnp.where(head0_qk, zero, q)], axis=0)
        s = lax.dot_general(q2, kwin, (((1,), (1,)), ((), ())), preferred_element_type=F32)
        no_history = jnp.logical_and(first_chunk, i == 0).astype(jnp.int32)
        s = s + bias_ref[no_history]
        m = jnp.max(s, axis=-1, keepdims=True)
        e = jnp.exp(s - m).astype(BF16)
        pv = jnp.dot(e, jnp.concatenate([vwin, ones], axis=1), preferred_element_type=F32)
        o_ref[0, pl.ds(r0, BLK), c:c + LANES] = jnp.where(
            head0, pv[:BLK, :LANES], pv[BLK:, :LANES]).astype(o_ref.dtype)
        m_both = jnp.where(head0, jnp.broadcast_to(m[:BLK], (BLK, LANES)), jnp.broadcast_to(m[BLK:], (BLK, LANES)))
        l_both = jnp.where(head0, pv[:BLK, LANES:], pv[BLK:, LANES:])
        stat_ref[0, pl.ds(r0, BLK), c:c + LANES] = jnp.where(stat_is_max, m_both, l_both)

    tiles = cols // LANES
    tiles_per_body = min(tiles, ATTN_UNITS)
    blocks_per_body = ATTN_UNITS // tiles_per_body
    for c0 in range(0, tiles, tiles_per_body):
        def body(ib, carry, c0=c0):
            for bi in range(blocks_per_body):
                for ci in range(tiles_per_body):
                    unit(ib * blocks_per_body + bi, (c0 + ci) * LANES)
            return carry
        lax.fori_loop(0, n_blk // blocks_per_body, body, 0)


def _attention(q, k, v, bias, dil):
    B, L, width = q.shape
    W = width // dil
    rows = min(L, max(ATTN_ROWS // dil, ATTN_MIN_ROWS))
    cols = (ATTN_ROWS // rows) * W
    hist_blocks = rows // BLK
    main = pl.BlockSpec((1, rows, cols), lambda b, n, r: (b, n, r))
    hist = pl.BlockSpec((1, BLK, cols), lambda b, n, r: (b, jnp.maximum(n * hist_blocks - 1, 0), r))
    return pl.pallas_call(
        _attn_kernel,
        grid=(B, L // rows, dil * W // cols),
        in_specs=[main, main, hist, main, hist, _const_spec(bias.shape)],
        out_specs=[main, main],
        out_shape=[jax.ShapeDtypeStruct((B, L, dil * W), BF16), jax.ShapeDtypeStruct((B, L, dil * W), F32)],
        scratch_shapes=[pltpu.VMEM((BLK + rows, cols), BF16), pltpu.VMEM((BLK + rows, cols), BF16)],
        compiler_params=pltpu.CompilerParams(
            dimension_semantics=("parallel", "arbitrary", "parallel"), vmem_limit_bytes=VMEM_LIMIT),
        name=f"attn_d{dil}",
    )(q, k, k, v, v, bias)


def _band_bias():
    a = np.arange(2 * BLK)[:, None] % BLK
    c = np.arange(2 * BLK)[None, :]
    band = (c >= a) & (c <= a + BLK)
    both = np.stack([band, band & (c >= BLK)])
    return jnp.asarray(np.where(both, 0.0, MASK_VALUE), F32)


def _stage_buffers(dil):
    return 0 if dil == 1 else 1 if dil <= SUBLANE_STRIDE else 2


def _natural_order(ref, dil, stage):
    if dil == 1:
        return ref[0].astype(F32)
    n = ref.shape[1]
    tm = n * dil
    tiles = GROUP_WIDTH // LANES
    dst, mid = stage
    first = min(dil, SUBLANE_STRIDE)
    second, part = dil // first, tm // first
    for t in range(tiles):
        for r in range(dil):
            rh, rl = divmod(r, first)
            c = r * GROUP_WIDTH + t * LANES
            rows = ref[0, :, c:c + LANES].astype(F32)
            if second == 1:
                dst[t, pl.ds(r, n, stride=dil), :] = rows
            else:
                mid[t, pl.ds(rl * part + rh, n, stride=second), :] = rows
        if second > 1:
            for rl in range(first):
                dst[t, pl.ds(rl, part, stride=first), :] = mid[t, rl * part:(rl + 1) * part, :]
    return jnp.concatenate([dst[t] for t in range(tiles)], axis=-1)


def _pool_mixer(tile_in_seq, u_ref, uh_ref, wpool_ref, pscale_ref):
    tm = u_ref.shape[1]
    hist = jnp.where(tile_in_seq == 0, 0.0, uh_ref[0])
    ub = jnp.concatenate([hist, u_ref[0]], axis=0)
    s2 = ub + pltpu.roll(ub, 1, 0)
    s4 = s2 + pltpu.roll(s2, 2, 0)
    s8 = s4 + pltpu.roll(s4, 4, 0)
    s16 = s8 + pltpu.roll(s8, 8, 0)
    t_head = jnp.where(tile_in_seq == 0, lax.broadcasted_iota(jnp.int32, (POOL_HALO, LANES), 0) + 1, POOL_HALO + 1)
    lane = lax.broadcasted_iota(jnp.int32, (tm, LANES), 1)
    low = lane < POOL_GC
    tile = lambda s, c: s[POOL_HALO:, c:c + LANES]

    def mean(s, c, w):
        x = tile(s, c)
        inv_head = 1.0 / jnp.minimum(t_head, w).astype(F32)
        return jnp.concatenate([x[:POOL_HALO] * inv_head, x[POOL_HALO:] * (1.0 / w)], axis=0)

    w2, w4, w8, w16 = POOL_WINDOWS
    y_lo = jnp.where(low, mean(s2, 0, w2), mean(s4, 0, w4)) - tile(ub, 0)
    y_hi = jnp.where(low, mean(s8, LANES, w8), mean(s16, LANES, w16)) - tile(ub, LANES)
    y = jnp.concatenate([y_lo, y_hi], axis=-1).astype(BF16)
    return jnp.dot(y, wpool_ref[...], preferred_element_type=F32) * pscale_ref[...]


def _group_scales(stat_refs, stages):
    stat = [_natural_order(r, d, stage) for r, d, stage in zip(stat_refs, DILATIONS, stages)]
    lane = lax.broadcasted_iota(jnp.int32, stat[0].shape, 1)
    is_max = (lane % HEAD_DIM) < STAT_SPLIT
    mx = jnp.maximum(jnp.maximum(stat[0], stat[1]), stat[2])
    e_on_max = [jnp.exp(s - mx) for s in stat]
    e_on_sum = [pltpu.roll(e, STAT_SPLIT, 1) for e in e_on_max]
    inv_on_sum = 1.0 / (stat[0] * e_on_sum[0] + stat[1] * e_on_sum[1] + stat[2] * e_on_sum[2])
    inv_on_max = pltpu.roll(inv_on_sum, GROUP_WIDTH - STAT_SPLIT, 1)
    return [jnp.where(is_max, a * inv_on_max, b * inv_on_sum) for a, b in zip(e_on_max, e_on_sum)]


def _mix_kernel(h_ref, u_ref, uh_ref, o1_ref, o4_ref, o16_ref, l1_ref, l4_ref, l16_ref, p_ref,
                wpool_ref, pscale_ref, wout_ref, g2_ref, wup_ref, wdown_ref, g3_ref, wgate_ref, wple_ref,
                gf_ref, out_ref, carry, *stages, final, tiles_per_seq, n_tiles):
    s = pl.program_id(0)
    D = h_ref.shape[2]
    tile_in_seq = jnp.minimum(s, n_tiles - 1) % tiles_per_seq
    o_refs, stat_refs = (o1_ref, o4_ref, o16_ref), (l1_ref, l4_ref, l16_ref)
    stage = iter(stages)
    take = lambda d: [next(stage) for _ in range(_stage_buffers(d))] + [None] * (2 - _stage_buffers(d))
    o_stages = [take(d) for d in DILATIONS]
    stat_stages = [take(d) for d in DILATIONS]

    def next_products():
        def product(val, row):
            return jnp.dot(val.astype(BF16), wout_ref[row:row + GROUP_WIDTH, :], preferred_element_type=F32)
        pool = product(_pool_mixer(tile_in_seq, u_ref, uh_ref, wpool_ref, pscale_ref), 0)
        scales = _group_scales(stat_refs, stat_stages)
        yield pool
        for g, dil in enumerate(DILATIONS):
            val = _natural_order(o_refs[g], dil, o_stages[g]) * scales[g]
            yield product(val, POOL_WIDTH + g * GROUP_WIDTH)

    @pl.when(s == 0)
    def _():
        carry[...] = sum(next_products())

    @pl.when(s > 0)
    def _():
        h = h_ref[0] + carry[...]
        ple = jnp.dot(p_ref[0, 0].astype(BF16), wple_ref[...], preferred_element_type=F32)
        hn = _rms(h, g2_ref[...]).astype(BF16)
        acc, nxt, products = h, 0.0, next_products()
        pieces_per_chunk = (len(DILATIONS) + 1) * FF_CHUNK // D_FF
        for c in range(0, D_FF, FF_CHUNK):
            up = jnp.dot(hn, wup_ref[:, c:c + FF_CHUNK], preferred_element_type=F32)
            act = jnp.square(jnp.maximum(up, 0.0)).astype(BF16)
            acc = acc + jnp.dot(act, wdown_ref[c:c + FF_CHUNK, :], preferred_element_type=F32)
            for _ in range(pieces_per_chunk):
                nxt = nxt + next(products)
        h = acc
        carry[...] = nxt

        hn = _rms(h, g3_ref[...]).astype(BF16)
        half_ple = 0.5 * ple
        base = h + half_ple
        pieces = []
        for c in range(0, D, GATE_CHUNK):
            half_z = jnp.dot(hn, wgate_ref[:, c:c + GATE_CHUNK], preferred_element_type=F32)
            pieces.append(base[:, c:c + GATE_CHUNK] + half_ple[:, c:c + GATE_CHUNK] * jnp.tanh(half_z))
        if final:
            ssq = sum(jnp.sum(x * x, axis=-1, keepdims=True) for x in pieces)
            inv = lax.rsqrt(ssq / D + EPS)
            pieces = [x * inv * gf_ref[:, c:c + GATE_CHUNK] for x, c in zip(pieces, range(0, D, GATE_CHUNK))]
        for x, c in zip(pieces, range(0, D, GATE_CHUNK)):
            out_ref[0, :, c:c + GATE_CHUNK] = x


def _mix(h, u, o, stats, p, layer, wpool, pscale, wout, g2, wup, wdown, g3, wgate, wple, gf, final):
    B, S, D = h.shape
    tm = ROW_TILE
    nt = S // tm
    n_tiles = B * nt
    ahead = lambda s: jnp.minimum(s, n_tiles - 1)
    behind = lambda s: jnp.maximum(s - 1, 0)
    row_b = lambda width: pl.BlockSpec((1, tm, width), lambda s: (behind(s) // nt, behind(s) % nt, 0))
    row_a = lambda width: pl.BlockSpec((1, tm, width), lambda s: (ahead(s) // nt, ahead(s) % nt, 0))
    halo = pl.BlockSpec(
        (1, POOL_HALO, POOL_WIDTH),
        lambda s: (ahead(s) // nt, jnp.maximum(ahead(s) % nt * (tm // POOL_HALO) - 1, 0), 0))
    strided = [pl.BlockSpec((1, tm // dil, dil * GROUP_WIDTH), lambda s: (ahead(s) // nt, ahead(s) % nt, 0))
               for dil in DILATIONS]
    p_spec = pl.BlockSpec((1, 1, tm, p.shape[-1]), lambda s: (layer, behind(s) // nt, behind(s) % nt, 0))
    weights = [wpool, pscale, wout, g2, wup, wdown, g3, wgate, wple, gf]
    return pl.pallas_call(
        functools.partial(_mix_kernel, final=final, tiles_per_seq=nt, n_tiles=n_tiles),
        grid=(n_tiles + 1,),
        in_specs=[row_b(D), row_a(POOL_WIDTH), halo] + strided * 2 + [p_spec]
                 + [_const_spec(w.shape[1:], layer) if w.ndim == 3 else _const_spec(w.shape) for w in weights],
        out_specs=row_b(D),
        out_shape=jax.ShapeDtypeStruct((B, S, D), F32),
        scratch_shapes=[pltpu.VMEM((tm, D), F32)]
                       + [pltpu.VMEM((GROUP_WIDTH // LANES, tm, LANES), F32)
                          for dil in DILATIONS * 2 for _ in range(_stage_buffers(dil))],
        compiler_params=pltpu.CompilerParams(
            dimension_semantics=("arbitrary",), vmem_limit_bytes=VMEM_LIMIT),
        name="mix_final" if final else "mix",
    )(h, u, u, *o, *stats, p, *weights)


def _rotary_parts(positions):
    inv_freq = ROPE_THETA ** (-jnp.arange(0, ROT_DIM, 2, dtype=F32) / ROT_DIM)
    ang = positions.astype(F32)[..., None] * inv_freq
    cs = jnp.concatenate([jnp.cos(ang), jnp.sin(ang)], axis=-1)
    hi = cs.astype(BF16)
    rest = cs - hi.astype(F32)
    mid = rest.astype(BF16)
    lo = (rest - mid.astype(F32)).astype(BF16)
    lane = np.arange(LANES)
    rotary = (lane % HEAD_DIM) < ROT_DIM
    pick = ((lane[None, :] % _HALF == np.arange(_HALF)[:, None]) & rotary[None, :]).astype(np.float32)
    sign = np.where(lane < HEAD_DIM, -1.0, 1.0).astype(np.float32)
    zero = np.zeros_like(pick)
    one_part = np.block([[pick, zero], [zero, pick * sign]])
    sel = np.concatenate([one_part] * 3, axis=0)
    return jnp.concatenate([hi, mid, lo], axis=-1), jnp.asarray(sel, BF16)


def _block_diag(w):
    g, c, _ = w.shape
    eye = jnp.eye(g, dtype=w.dtype)
    return (w[:, :, None, :] * eye[:, None, :, None]).reshape(g * c, g * c)


def kernel(x, p, positions, norm1, w_in, pool_w, pool_scale, w_out, norm2, w_up, w_down, norm3, w_gate,
           w_ple, final_norm):
    depth = w_in.shape[0]
    n_grp = len(DILATIONS)
    cs, sel = _rotary_parts(positions)
    bias = _band_bias()
    vec = lambda a: a.reshape(1, -1)
    w_in_b, w_out_b, w_up_b, w_down_b, w_gate_b, w_ple_b = _prepare_weights(
        w_in, (w_out, w_up, w_down, w_gate, w_ple), (1.0, 1.0, 1.0, 0.5, 1.0))
    h = x
    for i in range(depth):
        u, *qkv = _inproj(h, vec(norm1[i]), w_in_b, i, cs, sel)
        o, stats = [], []
        for g, dil in enumerate(DILATIONS):
            og, sg = _attention(qkv[g], qkv[n_grp + g], qkv[2 * n_grp + g], bias, dil)
            o.append(og)
            stats.append(sg)
        h = _mix(h, u, o, stats, p, i, _block_diag(pool_w[i]).astype(BF16), vec(pool_scale[i]),
                 w_out_b, vec(norm2[i]), w_up_b, w_down_b, vec(norm3[i]), w_gate_b, w_ple_b, vec(final_norm),
                 final=(i == depth - 1))
    return h
```

```python
import functools

import numpy as np
import jax
import jax.numpy as jnp
from jax import lax
from jax.experimental import pallas as pl
from jax.experimental.pallas import tpu as pltpu

F32 = jnp.float32
BF16 = jnp.bfloat16

D_MODEL = 1024
HEAD_DIM = 64
POOL_WIDTH = 256
POOL_WINDOWS = (2, 4, 8, 16)
POOL_GC = POOL_WIDTH // len(POOL_WINDOWS)
ATTN_WIDTH = D_MODEL - POOL_WIDTH
DILATIONS = (1, 4, 16)
GROUP_WIDTH = ATTN_WIDTH // len(DILATIONS)
ROT_DIM = HEAD_DIM // 4
ROPE_THETA = 500000.0
BLK = 128
D_FF = 4 * D_MODEL
EPS = 1e-6
MASK_VALUE = -1e30

LANES = 128
POOL_HALO = 16
ROW_TILE = 512
INPROJ_TILE = 1024
ATTN_ROWS = 4096
ATTN_MIN_ROWS = 512
ATTN_UNITS = 64
STAT_SPLIT = HEAD_DIM // 2
FF_CHUNK = 1024
GATE_CHUNK = 256
VMEM_LIMIT = 56 * 1024 * 1024
WEIGHT_SLABS = 8
SUBLANE_STRIDE = 4


def _rms(x, g):
    return x * lax.rsqrt(jnp.mean(x * x, axis=-1, keepdims=True) + EPS) * g


def _const_spec(shape, layer=None):
    zeros = (0,) * len(shape)
    if layer is None:
        return pl.BlockSpec(shape, lambda *_: zeros, pipeline_mode=pl.Buffered(1))
    return pl.BlockSpec((None,) + tuple(shape), lambda *_: (layer,) + zeros, pipeline_mode=pl.Buffered(1))


_HALF = ROT_DIM // 2
_QKV_ORDER = ((0, 2), (0, 0), (1, 2), (0, 1), (2, 2), (1, 0), (1, 1), (2, 0), (2, 1))


def _residue_rows(src, mid, dil):
    tm = src.shape[0]
    if dil <= SUBLANE_STRIDE:
        for r in range(dil):
            yield r, src[pl.ds(r, tm // dil, stride=dil), :]
        return
    first, second = SUBLANE_STRIDE, dil // SUBLANE_STRIDE
    part = tm // first
    for rl in range(first):
        mid[rl * part:(rl + 1) * part, :] = src[pl.ds(rl, part, stride=first), :]
    for r in range(dil):
        rh, rl = divmod(r, first)
        yield r, mid[pl.ds(rl * part + rh, tm // dil, stride=second), :]


def _prep_kernel(*refs, scales):
    n = len(refs) // 2
    (win_ref, *plain_in), (win_out, *plain_out) = refs[:n], refs[n:]
    for w_ref, o_ref, scale in zip(plain_in, plain_out, scales):
        w = w_ref[0]
        o_ref[0] = (w if scale == 1.0 else w * scale).astype(o_ref.dtype)
    lane = lax.broadcasted_iota(jnp.int32, (win_ref.shape[1], LANES), 1)
    swap = HEAD_DIM - _HALF
    from_high = (lane >= _HALF) & (lane < ROT_DIM)
    from_low = (lane >= HEAD_DIM) & (lane < HEAD_DIM + _HALF)
    for c in range(0, win_ref.shape[2], LANES):
        w = win_ref[0, :, c:c + LANES]
        if POOL_WIDTH <= c < POOL_WIDTH + 2 * ATTN_WIDTH:
            w = jnp.where(from_high, pltpu.roll(w, LANES - swap, 1), jnp.where(from_low, pltpu.roll(w, swap, 1), w))
        if POOL_WIDTH <= c < POOL_WIDTH + ATTN_WIDTH:
            w = w * HEAD_DIM ** -0.5
        win_out[0, :, c:c + LANES] = w.astype(win_out.dtype)


def _prepare_weights(w_in, plain, scales):
    weights = (w_in,) + tuple(plain)
    depth = w_in.shape[0]
    slab = lambda w: pl.BlockSpec((1, w.shape[1] // WEIGHT_SLABS, w.shape[2]), lambda l, i: (l, i, 0))
    return pl.pallas_call(
        functools.partial(_prep_kernel, scales=tuple(scales)),
        grid=(depth, WEIGHT_SLABS),
        in_specs=[slab(w) for w in weights],
        out_specs=[slab(w) for w in weights],
        out_shape=[jax.ShapeDtypeStruct(w.shape, BF16) for w in weights],
        compiler_params=pltpu.CompilerParams(
            dimension_semantics=("parallel", "parallel"), vmem_limit_bytes=VMEM_LIMIT),
        name="weight_prep",
    )(*weights)


def _head_a_lanes(lane):
    return (lane < _HALF) | ((lane >= ROT_DIM) & (lane < HEAD_DIM + _HALF))


def _inproj_kernel(x_ref, g_ref, w_ref, cs_ref, sel_ref, u_ref, *refs):
    n_grp = len(DILATIONS)
    qkv_refs, (hnbuf, zbuf, *stages) = refs[:3 * n_grp], refs[3 * n_grp:]
    tm = x_ref.shape[1]
    s = pl.program_id(0)

    def column(kind, g):
        return POOL_WIDTH + kind * ATTN_WIDTH + g * GROUP_WIDTH

    def prepare_next():
        hn_next = _rms(x_ref[0], g_ref[...]).astype(BF16)
        first = column(*_QKV_ORDER[0])
        zbuf[...] = jnp.dot(hn_next, w_ref[:, first:first + GROUP_WIDTH], preferred_element_type=F32)
        hnbuf[...] = hn_next

    @pl.when(s == 0)
    def _():
        prepare_next()

    @pl.when(s > 0)
    def _():
        table = jnp.dot(cs_ref[0], sel_ref[...], preferred_element_type=F32)
        lane = lax.broadcasted_iota(jnp.int32, (1, LANES), 1)
        cos_t = table[:, :LANES] + ((lane % HEAD_DIM) >= ROT_DIM).astype(F32)
        sin_t = table[:, LANES:]
        hn = hnbuf[...]

        for b, (kind, g) in enumerate(_QKV_ORDER):
            out_ref = qkv_refs[kind * n_grp + g]
            if b == 0:
                z = zbuf[...]
            else:
                col = column(kind, g)
                z = jnp.dot(hn, w_ref[:, col:col + GROUP_WIDTH], preferred_element_type=F32)
            halves = [z[:, c:c + LANES] for c in range(0, GROUP_WIDTH, LANES)]
            if kind < 2:
                halves = [z_h * cos_t + pltpu.roll(z_h, LANES // 2, 1) * sin_t for z_h in halves]
            dil = DILATIONS[g]
            if dil == 1:
                out_ref[0] = jnp.concatenate(halves, axis=-1).astype(out_ref.dtype)
                continue
            src, mid = stages[2 * (b % 2)], stages[2 * (b % 2) + 1]
            for t, z_h in enumerate(halves):
                src[t] = z_h
                for r, rows in _residue_rows(src.at[t], mid.at[t], dil):
                    c = r * GROUP_WIDTH + t * LANES
                    out_ref[0, :, c:c + LANES] = rows.astype(out_ref.dtype)
        u_ref[0] = jnp.dot(hn, w_ref[:, :POOL_WIDTH], preferred_element_type=F32)
        prepare_next()


def _inproj(x, g, w_in, layer, cs, sel):
    B, S, D = x.shape
    n_in = w_in.shape[2]
    tm = INPROJ_TILE
    nt = S // tm
    n_tiles = B * nt
    ahead = lambda s: jnp.minimum(s, n_tiles - 1)
    behind = lambda s: jnp.maximum(s - 1, 0)
    row = lambda width, tile: pl.BlockSpec((1, tm, width), lambda s: (tile(s) // nt, tile(s) % nt, 0))
    strided = lambda dil: pl.BlockSpec((1, tm // dil, dil * GROUP_WIDTH),
                                       lambda s: (behind(s) // nt, behind(s) % nt, 0))
    out_shape = [jax.ShapeDtypeStruct((B, S, POOL_WIDTH), F32)]
    out_shape += [jax.ShapeDtypeStruct((B, S // dil, dil * GROUP_WIDTH), BF16) for dil in DILATIONS] * 3
    return pl.pallas_call(
        _inproj_kernel,
        grid=(n_tiles + 1,),
        in_specs=[row(D, ahead), _const_spec((1, D)), _const_spec((D, n_in), layer), row(cs.shape[-1], behind),
                  _const_spec(sel.shape)],
        out_specs=[row(POOL_WIDTH, behind)] + [strided(dil) for dil in DILATIONS] * 3,
        out_shape=out_shape,
        scratch_shapes=[pltpu.VMEM((tm, D), BF16), pltpu.VMEM((tm, GROUP_WIDTH), F32)]
                       + [pltpu.VMEM((GROUP_WIDTH // LANES, tm, LANES), F32)] * 4,
        compiler_params=pltpu.CompilerParams(
            dimension_semantics=("arbitrary",), vmem_limit_bytes=VMEM_LIMIT),
        name="inproj",
    )(x, g, w_in, cs, sel)


def _attn_kernel(q_ref, k_ref, kh_ref, v_ref, vh_ref, bias_ref, o_ref, stat_ref, kbuf, vbuf):
    rows, cols = q_ref.shape[1], q_ref.shape[2]
    n_blk = rows // BLK
    first_chunk = pl.program_id(1) == 0

    kbuf[0:BLK] = kh_ref[0]
    kbuf[BLK:] = k_ref[0]
    vbuf[0:BLK] = vh_ref[0]
    vbuf[BLK:] = v_ref[0]

    lane = lax.broadcasted_iota(jnp.int32, (BLK, LANES), 1)
    head0 = lane < HEAD_DIM
    head0_qk = _head_a_lanes(lane)
    stat_is_max = (lane % HEAD_DIM) < STAT_SPLIT
    ones = jnp.ones((2 * BLK, LANES), BF16)

    def unit(i, c):
        r0 = pl.multiple_of(i * BLK, BLK)
        q = q_ref[0, pl.ds(r0, BLK), c:c + LANES]
        kwin = kbuf[pl.ds(r0, 2 * BLK), c:c + LANES]
        vwin = vbuf[pl.ds(r0, 2 * BLK), c:c + LANES]
        zero = jnp.zeros_like(q)
        q2 = jnp.concatenate([jnp.where(head0_qk, q, zero), jnp.where(head0_qk, zero, q)], axis=0)
        s = lax.dot_general(q2, kwin, (((1,), (1,)), ((), ())), preferred_element_type=F32)
        no_history = jnp.logical_and(first_chunk, i == 0).astype(jnp.int32)
        s = s + bias_ref[no_history]
        m = jnp.max(s, axis=-1, keepdims=True)
        e = jnp.exp(s - m).astype(BF16)
        pv = jnp.dot(e, jnp.concatenate([vwin, ones], axis=1), preferred_element_type=F32)
        o_ref[0, pl.ds(r0, BLK), c:c + LANES] = jnp.where(
            head0, pv[:BLK, :LANES], pv[BLK:, :LANES]).astype(o_ref.dtype)
        m_both = jnp.where(head0, jnp.broadcast_to(m[:BLK], (BLK, LANES)), jnp.broadcast_to(m[BLK:], (BLK, LANES)))
        l_both = jnp.where(head0, pv[:BLK, LANES:], pv[BLK:, LANES:])
        stat_ref[0, pl.ds(r0, BLK), c:c + LANES] = jnp.where(stat_is_max, m_both, l_both)

    tiles = cols // LANES
    tiles_per_body = min(tiles, ATTN_UNITS)
    blocks_per_body = ATTN_UNITS // tiles_per_body
    for c0 in range(0, tiles, tiles_per_body):
        def body(ib, carry, c0=c0):
            for bi in range(blocks_per_body):
                for ci in range(tiles_per_body):
                    unit(ib * blocks_per_body + bi, (c0 + ci) * LANES)
            return carry
        lax.fori_loop(0, n_blk // blocks_per_body, body, 0)


def _attention(q, k, v, bias, dil):
    B, L, width = q.shape
    W = width // dil
    rows = min(L, max(ATTN_ROWS // dil, ATTN_MIN_ROWS))
    cols = (ATTN_ROWS // rows) * W
    hist_blocks = rows // BLK
    main = pl.BlockSpec((1, rows, cols), lambda b, n, r: (b, n, r))
    hist = pl.BlockSpec((1, BLK, cols), lambda b, n, r: (b, jnp.maximum(n * hist_blocks - 1, 0), r))
    return pl.pallas_call(
        _attn_kernel,
        grid=(B, L // rows, dil * W // cols),
        in_specs=[main, main, hist, main, hist, _const_spec(bias.shape)],
        out_specs=[main, main],
        out_shape=[jax.ShapeDtypeStruct((B, L, dil * W), BF16), jax.ShapeDtypeStruct((B, L, dil * W), F32)],
        scratch_shapes=[pltpu.VMEM((BLK + rows, cols), BF16), pltpu.VMEM((BLK + rows, cols), BF16)],
        compiler_params=pltpu.CompilerParams(
            dimension_semantics=("parallel", "arbitrary", "parallel"), vmem_limit_bytes=VMEM_LIMIT),
        name=f"attn_d{dil}",
    )(q, k, k, v, v, bias)


def _band_bias():
    a = np.arange(2 * BLK)[:, None] % BLK
    c = np.arange(2 * BLK)[None, :]
    band = (c >= a) & (c <= a + BLK)
    both = np.stack([band, band & (c >= BLK)])
    return jnp.asarray(np.where(both, 0.0, MASK_VALUE), F32)


def _stage_buffers(dil):
    return 0 if dil == 1 else 1 if dil <= SUBLANE_STRIDE else 2


def _natural_order(ref, dil, stage):
    if dil == 1:
        return ref[0].astype(F32)
    n = ref.shape[1]
    tm = n * dil
    tiles = GROUP_WIDTH // LANES
    dst, mid = stage
    first = min(dil, SUBLANE_STRIDE)
    second, part = dil // first, tm // first
    for t in range(tiles):
        for r in range(dil):
            rh, rl = divmod(r, first)
            c = r * GROUP_WIDTH + t * LANES
            rows = ref[0, :, c:c + LANES].astype(F32)
            if second == 1:
                dst[t, pl.ds(r, n, stride=dil), :] = rows
            else:
                mid[t, pl.ds(rl * part + rh, n, stride=second), :] = rows
        if second > 1:
            for rl in range(first):
                dst[t, pl.ds(rl, part, stride=first), :] = mid[t, rl * part:(rl + 1) * part, :]
    return jnp.concatenate([dst[t] for t in range(tiles)], axis=-1)


def _pool_mixer(tile_in_seq, u_ref, uh_ref, wpool_ref, pscale_ref):
    tm = u_ref.shape[1]
    hist = jnp.where(tile_in_seq == 0, 0.0, uh_ref[0])
    ub = jnp.concatenate([hist, u_ref[0]], axis=0)
    s2 = ub + pltpu.roll(ub, 1, 0)
    s4 = s2 + pltpu.roll(s2, 2, 0)
    s8 = s4 + pltpu.roll(s4, 4, 0)
    s16 = s8 + pltpu.roll(s8, 8, 0)
    t_head = jnp.where(tile_in_seq == 0, lax.broadcasted_iota(jnp.int32, (POOL_HALO, LANES), 0) + 1, POOL_HALO + 1)
    lane = lax.broadcasted_iota(jnp.int32, (tm, LANES), 1)
    low = lane < POOL_GC
    tile = lambda s, c: s[POOL_HALO:, c:c + LANES]

    def mean(s, c, w):
        x = tile(s, c)
        inv_head = 1.0 / jnp.minimum(t_head, w).astype(F32)
        return jnp.concatenate([x[:POOL_HALO] * inv_head, x[POOL_HALO:] * (1.0 / w)], axis=0)

    w2, w4, w8, w16 = POOL_WINDOWS
    y_lo = jnp.where(low, mean(s2, 0, w2), mean(s4, 0, w4)) - tile(ub, 0)
    y_hi = jnp.where(low, mean(s8, LANES, w8), mean(s16, LANES, w16)) - tile(ub, LANES)
    y = jnp.concatenate([y_lo, y_hi], axis=-1).astype(BF16)
    return jnp.dot(y, wpool_ref[...], preferred_element_type=F32) * pscale_ref[...]


def _group_scales(stat_refs, stages):
    stat = [_natural_order(r, d, stage) for r, d, stage in zip(stat_refs, DILATIONS, stages)]
    lane = lax.broadcasted_iota(jnp.int32, stat[0].shape, 1)
    is_max = (lane % HEAD_DIM) < STAT_SPLIT
    mx = jnp.maximum(jnp.maximum(stat[0], stat[1]), stat[2])
    e_on_max = [jnp.exp(s - mx) for s in stat]
    e_on_sum = [pltpu.roll(e, STAT_SPLIT, 1) for e in e_on_max]
    inv_on_sum = 1.0 / (stat[0] * e_on_sum[0] + stat[1] * e_on_sum[1] + stat[2] * e_on_sum[2])
    inv_on_max = pltpu.roll(inv_on_sum, GROUP_WIDTH - STAT_SPLIT, 1)
    return [jnp.where(is_max, a * inv_on_max, b * inv_on_sum) for a, b in zip(e_on_max, e_on_sum)]


def _mix_kernel(h_ref, u_ref, uh_ref, o1_ref, o4_ref, o16_ref, l1_ref, l4_ref, l16_ref, p_ref,
                wpool_ref, pscale_ref, wout_ref, g2_ref, wup_ref, wdown_ref, g3_ref, wgate_ref, wple_ref,
                gf_ref, out_ref, carry, *stages, final, tiles_per_seq, n_tiles):
    s = pl.program_id(0)
    D = h_ref.shape[2]
    tile_in_seq = jnp.minimum(s, n_tiles - 1) % tiles_per_seq
    o_refs, stat_refs = (o1_ref, o4_ref, o16_ref), (l1_ref, l4_ref, l16_ref)
    stage = iter(stages)
    take = lambda d: [next(stage) for _ in range(_stage_buffers(d))] + [None] * (2 - _stage_buffers(d))
    o_stages = [take(d) for d in DILATIONS]
    stat_stages = [take(d) for d in DILATIONS]

    def next_products():
        def product(val, row):
            return jnp.dot(val.astype(BF16), wout_ref[row:row + GROUP_WIDTH, :], preferred_element_type=F32)
        pool = product(_pool_mixer(tile_in_seq, u_ref, uh_ref, wpool_ref, pscale_ref), 0)
        scales = _group_scales(stat_refs, stat_stages)
        yield pool
        for g, dil in enumerate(DILATIONS):
            val = _natural_order(o_refs[g], dil, o_stages[g]) * scales[g]
            yield product(val, POOL_WIDTH + g * GROUP_WIDTH)

    @pl.when(s == 0)
    def _():
        carry[...] = sum(next_products())

    @pl.when(s > 0)
    def _():
        h = h_ref[0] + carry[...]
        ple = jnp.dot(p_ref[0, 0].astype(BF16), wple_ref[...], preferred_element_type=F32)
        hn = _rms(h, g2_ref[...]).astype(BF16)
        acc, nxt, products = h, 0.0, next_products()
        pieces_per_chunk = (len(DILATIONS) + 1) * FF_CHUNK // D_FF
        for c in range(0, D_FF, FF_CHUNK):
            up = jnp.dot(hn, wup_ref[:, c:c + FF_CHUNK], preferred_element_type=F32)
            act = jnp.square(jnp.maximum(up, 0.0)).astype(BF16)
            acc = acc + jnp.dot(act, wdown_ref[c:c + FF_CHUNK, :], preferred_element_type=F32)
            for _ in range(pieces_per_chunk):
                nxt = nxt + next(products)
        h = acc
        carry[...] = nxt

        hn = _rms(h, g3_ref[...]).astype(BF16)
        half_ple = 0.5 * ple
        base = h + half_ple
        pieces = []
        for c in range(0, D, GATE_CHUNK):
            half_z = jnp.dot(hn, wgate_ref[:, c:c + GATE_CHUNK], preferred_element_type=F32)
            pieces.append(base[:, c:c + GATE_CHUNK] + half_ple[:, c:c + GATE_CHUNK] * jnp.tanh(half_z))
        if final:
            ssq = sum(jnp.sum(x * x, axis=-1, keepdims=True) for x in pieces)
            inv = lax.rsqrt(ssq / D + EPS)
            pieces = [x * inv * gf_ref[:, c:c + GATE_CHUNK] for x, c in zip(pieces, range(0, D, GATE_CHUNK))]
        for x, c in zip(pieces, range(0, D, GATE_CHUNK)):
            out_ref[0, :, c:c + GATE_CHUNK] = x


def _mix(h, u, o, stats, p, layer, wpool, pscale, wout, g2, wup, wdown, g3, wgate, wple, gf, final):
    B, S, D = h.shape
    tm = ROW_TILE
    nt = S // tm
    n_tiles = B * nt
    ahead = lambda s: jnp.minimum(s, n_tiles - 1)
    behind = lambda s: jnp.maximum(s - 1, 0)
    row_b = lambda width: pl.BlockSpec((1, tm, width), lambda s: (behind(s) // nt, behind(s) % nt, 0))
    row_a = lambda width: pl.BlockSpec((1, tm, width), lambda s: (ahead(s) // nt, ahead(s) % nt, 0))
    halo = pl.BlockSpec(
        (1, POOL_HALO, POOL_WIDTH),
        lambda s: (ahead(s) // nt, jnp.maximum(ahead(s) % nt * (tm // POOL_HALO) - 1, 0), 0))
    strided = [pl.BlockSpec((1, tm // dil, dil * GROUP_WIDTH), lambda s: (ahead(s) // nt, ahead(s) % nt, 0))
               for dil in DILATIONS]
    p_spec = pl.BlockSpec((1, 1, tm, p.shape[-1]), lambda s: (layer, behind(s) // nt, behind(s) % nt, 0))
    weights = [wpool, pscale, wout, g2, wup, wdown, g3, wgate, wple, gf]
    return pl.pallas_call(
        functools.partial(_mix_kernel, final=final, tiles_per_seq=nt, n_tiles=n_tiles),
        grid=(n_tiles + 1,),
        in_specs=[row_b(D), row_a(POOL_WIDTH), halo] + strided * 2 + [p_spec]
                 + [_const_spec(w.shape[1:], layer) if w.ndim == 3 else _const_spec(w.shape) for w in weights],
        out_specs=row_b(D),
        out_shape=jax.ShapeDtypeStruct((B, S, D), F32),
        scratch_shapes=[pltpu.VMEM((tm, D), F32)]
                       + [pltpu.VMEM((GROUP_WIDTH // LANES, tm, LANES), F32)
                          for dil in DILATIONS * 2 for _ in range(_stage_buffers(dil))],
        compiler_params=pltpu.CompilerParams(
            dimension_semantics=("arbitrary",), vmem_limit_bytes=VMEM_LIMIT),
        name="mix_final" if final else "mix",
    )(h, u, u, *o, *stats, p, *weights)


def _rotary_parts(positions):
    inv_freq = ROPE_THETA ** (-jnp.arange(0, ROT_DIM, 2, dtype=F32) / ROT_DIM)
    ang = positions.astype(F32)[..., None] * inv_freq
    cs = jnp.concatenate([jnp.cos(ang), jnp.sin(ang)], axis=-1)
    hi = cs.astype(BF16)
    rest = cs - hi.astype(F32)
    mid = rest.astype(BF16)
    lo = (rest - mid.astype(F32)).astype(BF16)
    lane = np.arange(LANES)
    rotary = (lane % HEAD_DIM) < ROT_DIM
    pick = ((lane[None, :] % _HALF == np.arange(_HALF)[:, None]) & rotary[None, :]).astype(np.float32)
    sign = np.where(lane < HEAD_DIM, -1.0, 1.0).astype(np.float32)
    zero = np.zeros_like(pick)
    one_part = np.block([[pick, zero], [zero, pick * sign]])
    sel = np.concatenate([one_part] * 3, axis=0)
    return jnp.concatenate([hi, mid, lo], axis=-1), jnp.asarray(sel, BF16)


def _block_diag(w):
    g, c, _ = w.shape
    eye = jnp.eye(g, dtype=w.dtype)
    return (w[:, :, None, :] * eye[:, None, :, None]).reshape(g * c, g * c)


def kernel(x, p, positions, norm1, w_in, pool_w, pool_scale, w_out, norm2, w_up, w_down, norm3, w_gate,
           w_ple, final_norm):
    depth = w_in.shape[0]
    n_grp = len(DILATIONS)
    cs, sel = _rotary_parts(positions)
    bias = _band_bias()
    vec = lambda a: a.reshape(1, -1)
    w_in_b, w_out_b, w_up_b, w_down_b, w_gate_b, w_ple_b = _prepare_weights(
        w_in, (w_out, w_up, w_down, w_gate, w_ple), (1.0, 1.0, 1.0, 0.5, 1.0))
    h = x
    for i in range(depth):
        u, *qkv = _inproj(h, vec(norm1[i]), w_in_b, i, cs, sel)
        o, stats = [], []
        for g, dil in enumerate(DILATIONS):
            og, sg = _attention(qkv[g], qkv[n_grp + g], qkv[2 * n_grp + g], bias, dil)
            o.append(og)
            stats.append(sg)
        h = _mix(h, u, o, stats, p, i, _block_diag(pool_w[i]).astype(BF16), vec(pool_scale[i]),
                 w_out_b, vec(norm2[i]), w_up_b, w_down_b, vec(norm3[i]), w_gate_b, w_ple_b, vec(final_norm),
                 final=(i == depth - 1))
    return h
```

```python
import functools

import numpy as np
import jax
import jax.numpy as jnp
from jax import lax
from jax.experimental import pallas as pl
from jax.experimental.pallas import tpu as pltpu

F32 = jnp.float32
BF16 = jnp.bfloat16

D_MODEL = 1024
HEAD_DIM = 64
POOL_WIDTH = 256
POOL_WINDOWS = (2, 4, 8, 16)
POOL_GC = POOL_WIDTH // len(POOL_WINDOWS)
ATTN_WIDTH = D_MODEL - POOL_WIDTH
DILATIONS = (1, 4, 16)
GROUP_WIDTH = ATTN_WIDTH // len(DILATIONS)
ROT_DIM = HEAD_DIM // 4
ROPE_THETA = 500000.0
BLK = 128
D_FF = 4 * D_MODEL
EPS = 1e-6
MASK_VALUE = -1e30

LANES = 128
POOL_HALO = 16
ROW_TILE = 512
INPROJ_TILE = 1024
ATTN_ROWS = 4096
ATTN_MIN_ROWS = 512
STAT_SPLIT = HEAD_DIM // 2
FF_CHUNK = 1024
GATE_CHUNK = 256
VMEM_LIMIT = 56 * 1024 * 1024
WEIGHT_SLABS = 8
SUBLANE_STRIDE = 4


def _rms(x, g):
    return x * lax.rsqrt(jnp.mean(x * x, axis=-1, keepdims=True) + EPS) * g


def _const_spec(shape, layer=None):
    zeros = (0,) * len(shape)
    if layer is None:
        return pl.BlockSpec(shape, lambda *_: zeros, pipeline_mode=pl.Buffered(1))
    return pl.BlockSpec((None,) + tuple(shape), lambda *_: (layer,) + zeros, pipeline_mode=pl.Buffered(1))


_HALF = ROT_DIM // 2
_QKV_ORDER = ((2, 2), (0, 0), (0, 2), (0, 1), (1, 2), (1, 0), (1, 1), (2, 0), (2, 1))


def _residue_rows(src, mid, dil):
    tm = src.shape[0]
    if dil <= SUBLANE_STRIDE:
        for r in range(dil):
            yield r, src[pl.ds(r, tm // dil, stride=dil), :]
        return
    first, second = SUBLANE_STRIDE, dil // SUBLANE_STRIDE
    part = tm // first
    for rl in range(first):
        mid[rl * part:(rl + 1) * part, :] = src[pl.ds(rl, part, stride=first), :]
    for r in range(dil):
        rh, rl = divmod(r, first)
        yield r, mid[pl.ds(rl * part + rh, tm // dil, stride=second), :]


def _prep_kernel(*refs, scales):
    n = len(refs) // 2
    (win_ref, *plain_in), (win_out, *plain_out) = refs[:n], refs[n:]
    for w_ref, o_ref, scale in zip(plain_in, plain_out, scales):
        w = w_ref[0]
        o_ref[0] = (w if scale == 1.0 else w * scale).astype(o_ref.dtype)
    lane = lax.broadcasted_iota(jnp.int32, (win_ref.shape[1], LANES), 1)
    swap = HEAD_DIM - _HALF
    from_high = (lane >= _HALF) & (lane < ROT_DIM)
    from_low = (lane >= HEAD_DIM) & (lane < HEAD_DIM + _HALF)
    for c in range(0, win_ref.shape[2], LANES):
        w = win_ref[0, :, c:c + LANES]
        if POOL_WIDTH <= c < POOL_WIDTH + 2 * ATTN_WIDTH:
            w = jnp.where(from_high, pltpu.roll(w, LANES - swap, 1), jnp.where(from_low, pltpu.roll(w, swap, 1), w))
        if POOL_WIDTH <= c < POOL_WIDTH + ATTN_WIDTH:
            w = w * HEAD_DIM ** -0.5
        win_out[0, :, c:c + LANES] = w.astype(win_out.dtype)


def _prepare_weights(w_in, plain, scales):
    weights = (w_in,) + tuple(plain)
    depth = w_in.shape[0]
    slab = lambda w: pl.BlockSpec((1, w.shape[1] // WEIGHT_SLABS, w.shape[2]), lambda l, i: (l, i, 0))
    return pl.pallas_call(
        functools.partial(_prep_kernel, scales=tuple(scales)),
        grid=(depth, WEIGHT_SLABS),
        in_specs=[slab(w) for w in weights],
        out_specs=[slab(w) for w in weights],
        out_shape=[jax.ShapeDtypeStruct(w.shape, BF16) for w in weights],
        compiler_params=pltpu.CompilerParams(
            dimension_semantics=("parallel", "parallel"), vmem_limit_bytes=VMEM_LIMIT),
        name="weight_prep",
    )(*weights)


def _head_a_lanes(lane):
    return (lane < _HALF) | ((lane >= ROT_DIM) & (lane < HEAD_DIM + _HALF))


def _inproj_kernel(x_ref, g_ref, w_ref, cs_ref, sel_ref, u_ref, *refs):
    n_grp = len(DILATIONS)
    qkv_refs, (hnbuf, zbuf, *stages) = refs[:3 * n_grp], refs[3 * n_grp:]
    tm = x_ref.shape[1]
    s = pl.program_id(0)

    def column(kind, g):
        return POOL_WIDTH + kind * ATTN_WIDTH + g * GROUP_WIDTH

    def prepare_next():
        hn_next = _rms(x_ref[0], g_ref[...]).astype(BF16)
        first = column(*_QKV_ORDER[0])
        zbuf[...] = jnp.dot(hn_next, w_ref[:, first:first + GROUP_WIDTH], preferred_element_type=F32)
        hnbuf[...] = hn_next

    @pl.when(s == 0)
    def _():
        prepare_next()

    @pl.when(s > 0)
    def _():
        table = jnp.dot(cs_ref[0], sel_ref[...], preferred_element_type=F32)
        lane = lax.broadcasted_iota(jnp.int32, (1, LANES), 1)
        cos_t = table[:, :LANES] + ((lane % HEAD_DIM) >= ROT_DIM).astype(F32)
        sin_t = table[:, LANES:]
        hn = hnbuf[...]

        for b, (kind, g) in enumerate(_QKV_ORDER):
            out_ref = qkv_refs[kind * n_grp + g]
            if b == 0:
                z = zbuf[...]
            else:
                col = column(kind, g)
                z = jnp.dot(hn, w_ref[:, col:col + GROUP_WIDTH], preferred_element_type=F32)
            halves = [z[:, c:c + LANES] for c in range(0, GROUP_WIDTH, LANES)]
            if kind < 2:
                halves = [z_h * cos_t + pltpu.roll(z_h, LANES // 2, 1) * sin_t for z_h in halves]
            dil = DILATIONS[g]
            if dil == 1:
                out_ref[0] = jnp.concatenate(halves, axis=-1).astype(out_ref.dtype)
                continue
            src, mid = stages[2 * (b % 2)], stages[2 * (b % 2) + 1]
            for t, z_h in enumerate(halves):
                src[t] = z_h
                for r, rows in _residue_rows(src.at[t], mid.at[t], dil):
                    c = r * GROUP_WIDTH + t * LANES
                    out_ref[0, :, c:c + LANES] = rows.astype(out_ref.dtype)
        u_ref[0] = jnp.dot(hn, w_ref[:, :POOL_WIDTH], preferred_element_type=F32)
        prepare_next()


def _inproj(x, g, w_in, layer, cs, sel):
    B, S, D = x.shape
    n_in = w_in.shape[2]
    tm = INPROJ_TILE
    nt = S // tm
    n_tiles = B * nt
    ahead = lambda s: jnp.minimum(s, n_tiles - 1)
    behind = lambda s: jnp.maximum(s - 1, 0)
    row = lambda width, tile: pl.BlockSpec((1, tm, width), lambda s: (tile(s) // nt, tile(s) % nt, 0))
    strided = lambda dil: pl.BlockSpec((1, tm // dil, dil * GROUP_WIDTH),
                                       lambda s: (behind(s) // nt, behind(s) % nt, 0))
    out_shape = [jax.ShapeDtypeStruct((B, S, POOL_WIDTH), F32)]
    out_shape += [jax.ShapeDtypeStruct((B, S // dil, dil * GROUP_WIDTH), BF16) for dil in DILATIONS] * 3
    return pl.pallas_call(
        _inproj_kernel,
        grid=(n_tiles + 1,),
        in_specs=[row(D, ahead), _const_spec((1, D)), _const_spec((D, n_in), layer), row(cs.shape[-1], behind),
                  _const_spec(sel.shape)],
        out_specs=[row(POOL_WIDTH, behind)] + [strided(dil) for dil in DILATIONS] * 3,
        out_shape=out_shape,
        scratch_shapes=[pltpu.VMEM((tm, D), BF16), pltpu.VMEM((tm, GROUP_WIDTH), F32)]
                       + [pltpu.VMEM((GROUP_WIDTH // LANES, tm, LANES), F32)] * 4,
        compiler_params=pltpu.CompilerParams(
            dimension_semantics=("arbitrary",), vmem_limit_bytes=VMEM_LIMIT),
        name="inproj",
    )(x, g, w_in, cs, sel)


def _attn_kernel(q_ref, k_ref, kh_ref, v_ref, vh_ref, bias_ref, o_ref, stat_ref, kfirst, vfirst):
    rows, cols = q_ref.shape[1], q_ref.shape[2]
    first_chunk = (pl.program_id(1) == 0).astype(jnp.int32)

    kfirst[:BLK] = kh_ref[0]
    kfirst[BLK:] = k_ref[0, :BLK]
    vfirst[:BLK] = vh_ref[0]
    vfirst[BLK:] = v_ref[0, :BLK]

    def window(ref, first, i, c):
        if i == 0:
            return first[:, c:c + LANES]
        return ref[0, (i - 1) * BLK:(i + 1) * BLK, c:c + LANES]

    lane = lax.broadcasted_iota(jnp.int32, (BLK, LANES), 1)
    head0 = lane < HEAD_DIM
    head0_qk = _head_a_lanes(lane)
    stat_is_max = (lane % HEAD_DIM) < STAT_SPLIT
    ones = jnp.ones((2 * BLK, LANES), BF16)

    def unit(i, c):
        r0 = i * BLK
        q = q_ref[0, r0:r0 + BLK, c:c + LANES]
        kwin = window(k_ref, kfirst, i, c)
        vwin = window(v_ref, vfirst, i, c)
        zero = jnp.zeros_like(q)
        q2 = jnp.concatenate([jnp.where(head0_qk, q, zero), jnp.where(head0_qk, zero, q)], axis=0)
        s = lax.dot_general(q2, kwin, (((1,), (1,)), ((), ())), preferred_element_type=F32)
        s = s + bias_ref[first_chunk if i == 0 else 0]
        m = jnp.max(s, axis=-1, keepdims=True)
        e = jnp.exp(s - m).astype(BF16)
        pv = jnp.dot(e, jnp.concatenate([vwin, ones], axis=1), preferred_element_type=F32)
        o_ref[0, r0:r0 + BLK, c:c + LANES] = jnp.where(
            head0, pv[:BLK, :LANES], pv[BLK:, :LANES]).astype(o_ref.dtype)
        m_both = jnp.where(head0, jnp.broadcast_to(m[:BLK], (BLK, LANES)), jnp.broadcast_to(m[BLK:], (BLK, LANES)))
        l_both = jnp.where(head0, pv[:BLK, LANES:], pv[BLK:, LANES:])
        stat_ref[0, r0:r0 + BLK, c:c + LANES] = jnp.where(stat_is_max, m_both, l_both)

    for i in range(rows // BLK):
        for c in range(0, cols, LANES):
            unit(i, c)


def _attention(q, k, v, bias, dil):
    B, L, width = q.shape
    W = width // dil
    rows = min(L, max(ATTN_ROWS // dil, ATTN_MIN_ROWS))
    cols = (ATTN_ROWS // rows) * W
    hist_blocks = rows // BLK
    main = pl.BlockSpec((1, rows, cols), lambda b, n, r: (b, n, r))
    hist = pl.BlockSpec((1, BLK, cols), lambda b, n, r: (b, jnp.maximum(n * hist_blocks - 1, 0), r))
    return pl.pallas_call(
        _attn_kernel,
        grid=(B, L // rows, dil * W // cols),
        in_specs=[main, main, hist, main, hist, _const_spec(bias.shape)],
        out_specs=[main, main],
        out_shape=[jax.ShapeDtypeStruct((B, L, dil * W), BF16), jax.ShapeDtypeStruct((B, L, dil * W), F32)],
        scratch_shapes=[pltpu.VMEM((2 * BLK, cols), BF16), pltpu.VMEM((2 * BLK, cols), BF16)],
        compiler_params=pltpu.CompilerParams(
            dimension_semantics=("parallel", "arbitrary", "parallel"), vmem_limit_bytes=VMEM_LIMIT),
        name=f"attn_d{dil}",
    )(q, k, k, v, v, bias)


def _band_bias():
    a = np.arange(2 * BLK)[:, None] % BLK
    c = np.arange(2 * BLK)[None, :]
    band = (c >= a) & (c <= a + BLK)
    both = np.stack([band, band & (c >= BLK)])
    return jnp.asarray(np.where(both, 0.0, MASK_VALUE), F32)


def _stage_buffers(dil):
    return 0 if dil == 1 else 1 if dil <= SUBLANE_STRIDE else 2


def _natural_order(ref, dil, stage):
    if dil == 1:
        return ref[0].astype(F32)
    n = ref.shape[1]
    tm = n * dil
    tiles = GROUP_WIDTH // LANES
    dst, mid = stage
    first = min(dil, SUBLANE_STRIDE)
    second, part = dil // first, tm // first
    for t in range(tiles):
        for r in range(dil):
            rh, rl = divmod(r, first)
            c = r * GROUP_WIDTH + t * LANES
            rows = ref[0, :, c:c + LANES].astype(F32)
            if second == 1:
                dst[t, pl.ds(r, n, stride=dil), :] = rows
            else:
                mid[t, pl.ds(rl * part + rh, n, stride=second), :] = rows
        if second > 1:
            for rl in range(first):
                dst[t, pl.ds(rl, part, stride=first), :] = mid[t, rl * part:(rl + 1) * part, :]
    return jnp.concatenate([dst[t] for t in range(tiles)], axis=-1)


def _pool_mixer(tile_in_seq, u_ref, uh_ref, wpool_ref, pscale_ref):
    tm = u_ref.shape[1]
    hist = jnp.where(tile_in_seq == 0, 0.0, uh_ref[0])
    ub = jnp.concatenate([hist, u_ref[0]], axis=0)
    s2 = ub + pltpu.roll(ub, 1, 0)
    s4 = s2 + pltpu.roll(s2, 2, 0)
    s8 = s4 + pltpu.roll(s4, 4, 0)
    s16 = s8 + pltpu.roll(s8, 8, 0)
    t_head = jnp.where(tile_in_seq == 0, lax.broadcasted_iota(jnp.int32, (POOL_HALO, LANES), 0) + 1, POOL_HALO + 1)
    lane = lax.broadcasted_iota(jnp.int32, (tm, LANES), 1)
    low = lane < POOL_GC
    tile = lambda s, c: s[POOL_HALO:, c:c + LANES]

    def mean(s, c, w):
        x = tile(s, c)
        inv_head = 1.0 / jnp.minimum(t_head, w).astype(F32)
        return jnp.concatenate([x[:POOL_HALO] * inv_head, x[POOL_HALO:] * (1.0 / w)], axis=0)

    w2, w4, w8, w16 = POOL_WINDOWS
    y_lo = jnp.where(low, mean(s2, 0, w2), mean(s4, 0, w4)) - tile(ub, 0)
    y_hi = jnp.where(low, mean(s8, LANES, w8), mean(s16, LANES, w16)) - tile(ub, LANES)
    y = jnp.concatenate([y_lo, y_hi], axis=-1).astype(BF16)
    return jnp.dot(y, wpool_ref[...], preferred_element_type=F32) * pscale_ref[...]


def _group_scales(stat_refs, stages):
    stat = [_natural_order(r, d, stage) for r, d, stage in zip(stat_refs, DILATIONS, stages)]
    lane = lax.broadcasted_iota(jnp.int32, stat[0].shape, 1)
    is_max = (lane % HEAD_DIM) < STAT_SPLIT
    mx = jnp.maximum(jnp.maximum(stat[0], stat[1]), stat[2])
    e_on_max = [jnp.exp(s - mx) for s in stat]
    e_on_sum = [pltpu.roll(e, STAT_SPLIT, 1) for e in e_on_max]
    inv_on_sum = 1.0 / (stat[0] * e_on_sum[0] + stat[1] * e_on_sum[1] + stat[2] * e_on_sum[2])
    inv_on_max = pltpu.roll(inv_on_sum, GROUP_WIDTH - STAT_SPLIT, 1)
    return [jnp.where(is_max, a * inv_on_max, b * inv_on_sum) for a, b in zip(e_on_max, e_on_sum)]


def _mix_kernel(h_ref, u_ref, uh_ref, o1_ref, o4_ref, o16_ref, l1_ref, l4_ref, l16_ref, p_ref,
                wpool_ref, pscale_ref, wout_ref, g2_ref, wup_ref, wdown_ref, g3_ref, wgate_ref, wple_ref,
                gf_ref, out_ref, carry, *stages, final, tiles_per_seq, n_tiles):
    s = pl.program_id(0)
    D = h_ref.shape[2]
    tile_in_seq = jnp.minimum(s, n_tiles - 1) % tiles_per_seq
    o_refs, stat_refs = (o1_ref, o4_ref, o16_ref), (l1_ref, l4_ref, l16_ref)
    stage = iter(stages)
    take = lambda d: [next(stage) for _ in range(_stage_buffers(d))] + [None] * (2 - _stage_buffers(d))
    o_stages = [take(d) for d in DILATIONS]
    stat_stages = [take(d) for d in DILATIONS]

    def next_products():
        def product(val, row):
            return jnp.dot(val.astype(BF16), wout_ref[row:row + GROUP_WIDTH, :], preferred_element_type=F32)
        pool = product(_pool_mixer(tile_in_seq, u_ref, uh_ref, wpool_ref, pscale_ref), 0)
        scales = _group_scales(stat_refs, stat_stages)
        yield pool
        for g, dil in enumerate(DILATIONS):
            val = _natural_order(o_refs[g], dil, o_stages[g]) * scales[g]
            yield product(val, POOL_WIDTH + g * GROUP_WIDTH)

    @pl.when(s == 0)
    def _():
        carry[...] = sum(next_products())

    @pl.when(s > 0)
    def _():
        h = h_ref[0] + carry[...]
        ple = jnp.dot(p_ref[0, 0].astype(BF16), wple_ref[...], preferred_element_type=F32)
        hn = _rms(h, g2_ref[...]).astype(BF16)
        acc, nxt, products = h, 0.0, next_products()
        pieces_per_chunk = (len(DILATIONS) + 1) * FF_CHUNK // D_FF
        for c in range(0, D_FF, FF_CHUNK):
            up = jnp.dot(hn, wup_ref[:, c:c + FF_CHUNK], preferred_element_type=F32)
            act = jnp.square(jnp.maximum(up, 0.0)).astype(BF16)
            acc = acc + jnp.dot(act, wdown_ref[c:c + FF_CHUNK, :], preferred_element_type=F32)
            for _ in range(pieces_per_chunk):
                nxt = nxt + next(products)
        h = acc
        carry[...] = nxt

        hn = _rms(h, g3_ref[...]).astype(BF16)
        half_ple = 0.5 * ple
        base = h + half_ple
        pieces = []
        for c in range(0, D, GATE_CHUNK):
            half_z = jnp.dot(hn, wgate_ref[:, c:c + GATE_CHUNK], preferred_element_type=F32)
            pieces.append(base[:, c:c + GATE_CHUNK] + half_ple[:, c:c + GATE_CHUNK] * jnp.tanh(half_z))
        if final:
            ssq = sum(jnp.sum(x * x, axis=-1, keepdims=True) for x in pieces)
            inv = lax.rsqrt(ssq / D + EPS)
            pieces = [x * inv * gf_ref[:, c:c + GATE_CHUNK] for x, c in zip(pieces, range(0, D, GATE_CHUNK))]
        for x, c in zip(pieces, range(0, D, GATE_CHUNK)):
            out_ref[0, :, c:c + GATE_CHUNK] = x


def _mix(h, u, o, stats, p, layer, wpool, pscale, wout, g2, wup, wdown, g3, wgate, wple, gf, final):
    B, S, D = h.shape
    tm = ROW_TILE
    nt = S // tm
    n_tiles = B * nt
    ahead = lambda s: jnp.minimum(s, n_tiles - 1)
    behind = lambda s: jnp.maximum(s - 1, 0)
    row_b = lambda width: pl.BlockSpec((1, tm, width), lambda s: (behind(s) // nt, behind(s) % nt, 0))
    row_a = lambda width: pl.BlockSpec((1, tm, width), lambda s: (ahead(s) // nt, ahead(s) % nt, 0))
    halo = pl.BlockSpec(
        (1, POOL_HALO, POOL_WIDTH),
        lambda s: (ahead(s) // nt, jnp.maximum(ahead(s) % nt * (tm // POOL_HALO) - 1, 0), 0))
    strided = [pl.BlockSpec((1, tm // dil, dil * GROUP_WIDTH), lambda s: (ahead(s) // nt, ahead(s) % nt, 0))
               for dil in DILATIONS]
    p_spec = pl.BlockSpec((1, 1, tm, p.shape[-1]), lambda s: (layer, behind(s) // nt, behind(s) % nt, 0))
    weights = [wpool, pscale, wout, g2, wup, wdown, g3, wgate, wple, gf]
    return pl.pallas_call(
        functools.partial(_mix_kernel, final=final, tiles_per_seq=nt, n_tiles=n_tiles),
        grid=(n_tiles + 1,),
        in_specs=[row_b(D), row_a(POOL_WIDTH), halo] + strided * 2 + [p_spec]
                 + [_const_spec(w.shape[1:], layer) if w.ndim == 3 else _const_spec(w.shape) for w in weights],
        out_specs=row_b(D),
        out_shape=jax.ShapeDtypeStruct((B, S, D), F32),
        scratch_shapes=[pltpu.VMEM((tm, D), F32)]
                       + [pltpu.VMEM((GROUP_WIDTH // LANES, tm, LANES), F32)
                          for dil in DILATIONS * 2 for _ in range(_stage_buffers(dil))],
        compiler_params=pltpu.CompilerParams(
            dimension_semantics=("arbitrary",), vmem_limit_bytes=VMEM_LIMIT),
        name="mix_final" if final else "mix",
    )(h, u, u, *o, *stats, p, *weights)


def _rotary_parts(positions):
    inv_freq = ROPE_THETA ** (-jnp.arange(0, ROT_DIM, 2, dtype=F32) / ROT_DIM)
    ang = positions.astype(F32)[..., None] * inv_freq
    cs = jnp.concatenate([jnp.cos(ang), jnp.sin(ang)], axis=-1)
    hi = cs.astype(BF16)
    rest = cs - hi.astype(F32)
    mid = rest.astype(BF16)
    lo = (rest - mid.astype(F32)).astype(BF16)
    lane = np.arange(LANES)
    rotary = (lane % HEAD_DIM) < ROT_DIM
    pick = ((lane[None, :] % _HALF == np.arange(_HALF)[:, None]) & rotary[None, :]).astype(np.float32)
    sign = np.where(lane < HEAD_DIM, -1.0, 1.0).astype(np.float32)
    zero = np.zeros_like(pick)
    one_part = np.block([[pick, zero], [zero, pick * sign]])
    sel = np.concatenate([one_part] * 3, axis=0)
    return jnp.concatenate([hi, mid, lo], axis=-1), jnp.asarray(sel, BF16)


def _block_diag(w):
    g, c, _ = w.shape
    eye = jnp.eye(g, dtype=w.dtype)
    return (w[:, :, None, :] * eye[:, None, :, None]).reshape(g * c, g * c)


def kernel(x, p, positions, norm1, w_in, pool_w, pool_scale, w_out, norm2, w_up, w_down, norm3, w_gate,
           w_ple, final_norm):
    depth = w_in.shape[0]
    n_grp = len(DILATIONS)
    cs, sel = _rotary_parts(positions)
    bias = _band_bias()
    vec = lambda a: a.reshape(1, -1)
    w_in_b, w_out_b, w_up_b, w_down_b, w_gate_b, w_ple_b = _prepare_weights(
        w_in, (w_out, w_up, w_down, w_gate, w_ple), (1.0, 1.0, 1.0, 0.5, 1.0))
    h = x
    for i in range(depth):
        u, *qkv = _inproj(h, vec(norm1[i]), w_in_b, i, cs, sel)
        o, stats = [], []
        for g, dil in enumerate(DILATIONS):
            og, sg = _attention(qkv[g], qkv[n_grp + g], qkv[2 * n_grp + g], bias, dil)
            o.append(og)
            stats.append(sg)
        h = _mix(h, u, o, stats, p, i, _block_diag(pool_w[i]).astype(BF16), vec(pool_scale[i]),
                 w_out_b, vec(norm2[i]), w_up_b, w_down_b, vec(norm3[i]), w_gate_b, w_ple_b, vec(final_norm),
                 final=(i == depth - 1))
    return h
```

```python
import functools

import numpy as np
import jax
import jax.numpy as jnp
from jax import lax
from jax.experimental import pallas as pl
from jax.experimental.pallas import tpu as pltpu

F32 = jnp.float32
BF16 = jnp.bfloat16

D_MODEL = 1024
HEAD_DIM = 64
POOL_WIDTH = 256
POOL_WINDOWS = (2, 4, 8, 16)
POOL_GC = POOL_WIDTH // len(POOL_WINDOWS)
ATTN_WIDTH = D_MODEL - POOL_WIDTH
DILATIONS = (1, 4, 16)
GROUP_WIDTH = ATTN_WIDTH // len(DILATIONS)
ROT_DIM = HEAD_DIM // 4
ROPE_THETA = 500000.0
BLK = 128
D_FF = 4 * D_MODEL
EPS = 1e-6
MASK_VALUE = -1e30

LANES = 128
POOL_HALO = 16
ROW_TILE = 512
INPROJ_TILE = 1024
ATTN_ROWS = 4096
ATTN_MIN_ROWS = 512
STAT_SPLIT = HEAD_DIM // 2
FF_CHUNK = 1024
GATE_CHUNK = 256
VMEM_LIMIT = 56 * 1024 * 1024
WEIGHT_SLABS = 8
SUBLANE_STRIDE = 4


def _rms(x, g):
    return x * lax.rsqrt(jnp.mean(x * x, axis=-1, keepdims=True) + EPS) * g


def _const_spec(shape, layer=None):
    zeros = (0,) * len(shape)
    if layer is None:
        return pl.BlockSpec(shape, lambda *_: zeros, pipeline_mode=pl.Buffered(1))
    return pl.BlockSpec((None,) + tuple(shape), lambda *_: (layer,) + zeros, pipeline_mode=pl.Buffered(1))


_HALF = ROT_DIM // 2
_QKV_ORDER = ((2, 2), (2, 1), (0, 0), (0, 2), (0, 1), (1, 2), (1, 0), (1, 1), (2, 0))
BLOCKS_AHEAD = 2


def _residue_rows(src, mid, dil):
    tm = src.shape[0]
    if dil <= SUBLANE_STRIDE:
        for r in range(dil):
            yield r, src[pl.ds(r, tm // dil, stride=dil), :]
        return
    first, second = SUBLANE_STRIDE, dil // SUBLANE_STRIDE
    part = tm // first
    for rl in range(first):
        mid[rl * part:(rl + 1) * part, :] = src[pl.ds(rl, part, stride=first), :]
    for r in range(dil):
        rh, rl = divmod(r, first)
        yield r, mid[pl.ds(rl * part + rh, tm // dil, stride=second), :]


def _prep_kernel(*refs, scales):
    n = len(refs) // 2
    (win_ref, *plain_in), (win_out, *plain_out) = refs[:n], refs[n:]
    for w_ref, o_ref, scale in zip(plain_in, plain_out, scales):
        w = w_ref[0]
        o_ref[0] = (w if scale == 1.0 else w * scale).astype(o_ref.dtype)
    lane = lax.broadcasted_iota(jnp.int32, (win_ref.shape[1], LANES), 1)
    swap = HEAD_DIM - _HALF
    from_high = (lane >= _HALF) & (lane < ROT_DIM)
    from_low = (lane >= HEAD_DIM) & (lane < HEAD_DIM + _HALF)
    for c in range(0, win_ref.shape[2], LANES):
        w = win_ref[0, :, c:c + LANES]
        if POOL_WIDTH <= c < POOL_WIDTH + 2 * ATTN_WIDTH:
            w = jnp.where(from_high, pltpu.roll(w, LANES - swap, 1), jnp.where(from_low, pltpu.roll(w, swap, 1), w))
        if POOL_WIDTH <= c < POOL_WIDTH + ATTN_WIDTH:
            w = w * HEAD_DIM ** -0.5
        win_out[0, :, c:c + LANES] = w.astype(win_out.dtype)


def _prepare_weights(w_in, plain, scales):
    weights = (w_in,) + tuple(plain)
    depth = w_in.shape[0]
    slab = lambda w: pl.BlockSpec((1, w.shape[1] // WEIGHT_SLABS, w.shape[2]), lambda l, i: (l, i, 0))
    return pl.pallas_call(
        functools.partial(_prep_kernel, scales=tuple(scales)),
        grid=(depth, WEIGHT_SLABS),
        in_specs=[slab(w) for w in weights],
        out_specs=[slab(w) for w in weights],
        out_shape=[jax.ShapeDtypeStruct(w.shape, BF16) for w in weights],
        compiler_params=pltpu.CompilerParams(
            dimension_semantics=("parallel", "parallel"), vmem_limit_bytes=VMEM_LIMIT),
        name="weight_prep",
    )(*weights)


def _head_a_lanes(lane):
    return (lane < _HALF) | ((lane >= ROT_DIM) & (lane < HEAD_DIM + _HALF))


def _inproj_kernel(x_ref, g_ref, w_ref, cs_ref, sel_ref, u_ref, *refs):
    n_grp = len(DILATIONS)
    qkv_refs, (hnbuf, *rest) = refs[:3 * n_grp], refs[3 * n_grp:]
    zbufs, stages = rest[:BLOCKS_AHEAD], rest[BLOCKS_AHEAD:]
    tm = x_ref.shape[1]
    s = pl.program_id(0)

    def column(kind, g):
        return POOL_WIDTH + kind * ATTN_WIDTH + g * GROUP_WIDTH

    def prepare_next():
        hn_next = _rms(x_ref[0], g_ref[...]).astype(BF16)
        for ahead_ref, block in zip(zbufs, _QKV_ORDER):
            col = column(*block)
            ahead_ref[...] = jnp.dot(hn_next, w_ref[:, col:col + GROUP_WIDTH], preferred_element_type=F32)
        hnbuf[...] = hn_next

    @pl.when(s == 0)
    def _():
        prepare_next()

    @pl.when(s > 0)
    def _():
        table = jnp.dot(cs_ref[0], sel_ref[...], preferred_element_type=F32)
        lane = lax.broadcasted_iota(jnp.int32, (1, LANES), 1)
        cos_t = table[:, :LANES] + ((lane % HEAD_DIM) >= ROT_DIM).astype(F32)
        sin_t = table[:, LANES:]
        hn = hnbuf[...]

        for b, (kind, g) in enumerate(_QKV_ORDER):
            out_ref = qkv_refs[kind * n_grp + g]
            if b < len(zbufs):
                z = zbufs[b][...]
            else:
                col = column(kind, g)
                z = jnp.dot(hn, w_ref[:, col:col + GROUP_WIDTH], preferred_element_type=F32)
            halves = [z[:, c:c + LANES] for c in range(0, GROUP_WIDTH, LANES)]
            if kind < 2:
                halves = [z_h * cos_t + pltpu.roll(z_h, LANES // 2, 1) * sin_t for z_h in halves]
            dil = DILATIONS[g]
            if dil == 1:
                out_ref[0] = jnp.concatenate(halves, axis=-1).astype(out_ref.dtype)
                continue
            src, mid = stages[2 * (b % 2)], stages[2 * (b % 2) + 1]
            for t, z_h in enumerate(halves):
                src[t] = z_h
                for r, rows in _residue_rows(src.at[t], mid.at[t], dil):
                    c = r * GROUP_WIDTH + t * LANES
                    out_ref[0, :, c:c + LANES] = rows.astype(out_ref.dtype)
        u_ref[0] = jnp.dot(hn, w_ref[:, :POOL_WIDTH], preferred_element_type=F32)
        prepare_next()


def _inproj(x, g, w_in, layer, cs, sel):
    B, S, D = x.shape
    n_in = w_in.shape[2]
    tm = INPROJ_TILE
    nt = S // tm
    n_tiles = B * nt
    ahead = lambda s: jnp.minimum(s, n_tiles - 1)
    behind = lambda s: jnp.maximum(s - 1, 0)
    row = lambda width, tile: pl.BlockSpec((1, tm, width), lambda s: (tile(s) // nt, tile(s) % nt, 0))
    strided = lambda dil: pl.BlockSpec((1, tm // dil, dil * GROUP_WIDTH),
                                       lambda s: (behind(s) // nt, behind(s) % nt, 0))
    out_shape = [jax.ShapeDtypeStruct((B, S, POOL_WIDTH), F32)]
    out_shape += [jax.ShapeDtypeStruct((B, S // dil, dil * GROUP_WIDTH), BF16) for dil in DILATIONS] * 3
    return pl.pallas_call(
        _inproj_kernel,
        grid=(n_tiles + 1,),
        in_specs=[row(D, ahead), _const_spec((1, D)), _const_spec((D, n_in), layer), row(cs.shape[-1], behind),
                  _const_spec(sel.shape)],
        out_specs=[row(POOL_WIDTH, behind)] + [strided(dil) for dil in DILATIONS] * 3,
        out_shape=out_shape,
        scratch_shapes=[pltpu.VMEM((tm, D), BF16)] + [pltpu.VMEM((tm, GROUP_WIDTH), F32)] * BLOCKS_AHEAD
                       + [pltpu.VMEM((GROUP_WIDTH // LANES, tm, LANES), F32)] * 4,
        compiler_params=pltpu.CompilerParams(
            dimension_semantics=("arbitrary",), vmem_limit_bytes=VMEM_LIMIT),
        name="inproj",
    )(x, g, w_in, cs, sel)


def _attn_kernel(q_ref, k_ref, kh_ref, v_ref, vh_ref, bias_ref, o_ref, stat_ref, kfirst, vfirst):
    rows, cols = q_ref.shape[1], q_ref.shape[2]
    first_chunk = (pl.program_id(1) == 0).astype(jnp.int32)

    kfirst[:BLK] = kh_ref[0]
    kfirst[BLK:] = k_ref[0, :BLK]
    vfirst[:BLK] = vh_ref[0]
    vfirst[BLK:] = v_ref[0, :BLK]

    def window(ref, first, i, c):
        if i == 0:
            return first[:, c:c + LANES]
        return ref[0, (i - 1) * BLK:(i + 1) * BLK, c:c + LANES]

    lane = lax.broadcasted_iota(jnp.int32, (BLK, LANES), 1)
    head0 = lane < HEAD_DIM
    head0_qk = _head_a_lanes(lane)
    stat_is_max = (lane % HEAD_DIM) < STAT_SPLIT
    ones = jnp.ones((2 * BLK, LANES), BF16)

    def unit(i, c):
        r0 = i * BLK
        q = q_ref[0, r0:r0 + BLK, c:c + LANES]
        kwin = window(k_ref, kfirst, i, c)
        vwin = window(v_ref, vfirst, i, c)
        zero = jnp.zeros_like(q)
        q2 = jnp.concatenate([jnp.where(head0_qk, q, zero), jnp.where(head0_qk, zero, q)], axis=0)
        s = lax.dot_general(q2, kwin, (((1,), (1,)), ((), ())), preferred_element_type=F32)
        s = s + bias_ref[first_chunk if i == 0 else 0]
        m = jnp.max(s, axis=-1, keepdims=True)
        e = jnp.exp(s - m).astype(BF16)
        pv = jnp.dot(e, jnp.concatenate([vwin, ones], axis=1), preferred_element_type=F32)
        o_ref[0, r0:r0 + BLK, c:c + LANES] = jnp.where(
            head0, pv[:BLK, :LANES], pv[BLK:, :LANES]).astype(o_ref.dtype)
        m_both = jnp.where(head0, jnp.broadcast_to(m[:BLK], (BLK, LANES)), jnp.broadcast_to(m[BLK:], (BLK, LANES)))
        l_both = jnp.where(head0, pv[:BLK, LANES:], pv[BLK:, LANES:])
        stat_ref[0, r0:r0 + BLK, c:c + LANES] = jnp.where(stat_is_max, m_both, l_both)

    for i in range(rows // BLK):
        for c in range(0, cols, LANES):
            unit(i, c)


def _attention(q, k, v, bias, dil):
    B, L, width = q.shape
    W = width // dil
    rows = min(L, max(ATTN_ROWS // dil, ATTN_MIN_ROWS))
    cols = (ATTN_ROWS // rows) * W
    hist_blocks = rows // BLK
    main = pl.BlockSpec((1, rows, cols), lambda b, n, r: (b, n, r))
    hist = pl.BlockSpec((1, BLK, cols), lambda b, n, r: (b, jnp.maximum(n * hist_blocks - 1, 0), r))
    return pl.pallas_call(
        _attn_kernel,
        grid=(B, L // rows, dil * W // cols),
        in_specs=[main, main, hist, main, hist, _const_spec(bias.shape)],
        out_specs=[main, main],
        out_shape=[jax.ShapeDtypeStruct((B, L, dil * W), BF16), jax.ShapeDtypeStruct((B, L, dil * W), F32)],
        scratch_shapes=[pltpu.VMEM((2 * BLK, cols), BF16), pltpu.VMEM((2 * BLK, cols), BF16)],
        compiler_params=pltpu.CompilerParams(
            dimension_semantics=("parallel", "arbitrary", "parallel"), vmem_limit_bytes=VMEM_LIMIT),
        name=f"attn_d{dil}",
    )(q, k, k, v, v, bias)


def _band_bias():
    a = np.arange(2 * BLK)[:, None] % BLK
    c = np.arange(2 * BLK)[None, :]
    band = (c >= a) & (c <= a + BLK)
    both = np.stack([band, band & (c >= BLK)])
    return jnp.asarray(np.where(both, 0.0, MASK_VALUE), F32)


def _stage_buffers(dil):
    return 0 if dil == 1 else 1 if dil <= SUBLANE_STRIDE else 2


def _natural_order(ref, dil, stage):
    if dil == 1:
        return ref[0].astype(F32)
    n = ref.shape[1]
    tm = n * dil
    tiles = GROUP_WIDTH // LANES
    dst, mid = stage
    first = min(dil, SUBLANE_STRIDE)
    second, part = dil // first, tm // first
    for t in range(tiles):
        for r in range(dil):
            rh, rl = divmod(r, first)
            c = r * GROUP_WIDTH + t * LANES
            rows = ref[0, :, c:c + LANES].astype(F32)
            if second == 1:
                dst[t, pl.ds(r, n, stride=dil), :] = rows
            else:
                mid[t, pl.ds(rl * part + rh, n, stride=second), :] = rows
        if second > 1:
            for rl in range(first):
                dst[t, pl.ds(rl, part, stride=first), :] = mid[t, rl * part:(rl + 1) * part, :]
    return jnp.concatenate([dst[t] for t in range(tiles)], axis=-1)


def _pool_mixer(tile_in_seq, u_ref, uh_ref, wpool_ref, pscale_ref):
    tm = u_ref.shape[1]
    hist = jnp.where(tile_in_seq == 0, 0.0, uh_ref[0])
    ub = jnp.concatenate([hist, u_ref[0]], axis=0)
    s2 = ub + pltpu.roll(ub, 1, 0)
    s4 = s2 + pltpu.roll(s2, 2, 0)
    s8 = s4 + pltpu.roll(s4, 4, 0)
    s16 = s8 + pltpu.roll(s8, 8, 0)
    t_head = jnp.where(tile_in_seq == 0, lax.broadcasted_iota(jnp.int32, (POOL_HALO, LANES), 0) + 1, POOL_HALO + 1)
    lane = lax.broadcasted_iota(jnp.int32, (tm, LANES), 1)
    low = lane < POOL_GC
    tile = lambda s, c: s[POOL_HALO:, c:c + LANES]

    def mean(s, c, w):
        x = tile(s, c)
        inv_head = 1.0 / jnp.minimum(t_head, w).astype(F32)
        return jnp.concatenate([x[:POOL_HALO] * inv_head, x[POOL_HALO:] * (1.0 / w)], axis=0)

    w2, w4, w8, w16 = POOL_WINDOWS
    y_lo = jnp.where(low, mean(s2, 0, w2), mean(s4, 0, w4)) - tile(ub, 0)
    y_hi = jnp.where(low, mean(s8, LANES, w8), mean(s16, LANES, w16)) - tile(ub, LANES)
    y = jnp.concatenate([y_lo, y_hi], axis=-1).astype(BF16)
    return jnp.dot(y, wpool_ref[...], preferred_element_type=F32) * pscale_ref[...]


def _group_scales(stat_refs, stages):
    stat = [_natural_order(r, d, stage) for r, d, stage in zip(stat_refs, DILATIONS, stages)]
    lane = lax.broadcasted_iota(jnp.int32, stat[0].shape, 1)
    is_max = (lane % HEAD_DIM) < STAT_SPLIT
    mx = jnp.maximum(jnp.maximum(stat[0], stat[1]), stat[2])
    e_on_max = [jnp.exp(s - mx) for s in stat]
    e_on_sum = [pltpu.roll(e, STAT_SPLIT, 1) for e in e_on_max]
    inv_on_sum = 1.0 / (stat[0] * e_on_sum[0] + stat[1] * e_on_sum[1] + stat[2] * e_on_sum[2])
    inv_on_max = pltpu.roll(inv_on_sum, GROUP_WIDTH - STAT_SPLIT, 1)
    return [jnp.where(is_max, a * inv_on_max, b * inv_on_sum) for a, b in zip(e_on_max, e_on_sum)]


def _mix_kernel(h_ref, u_ref, uh_ref, o1_ref, o4_ref, o16_ref, l1_ref, l4_ref, l16_ref, p_ref,
                wpool_ref, pscale_ref, wout_ref, g2_ref, wup_ref, wdown_ref, g3_ref, wgate_ref, wple_ref,
                gf_ref, out_ref, carry, *stages, final, tiles_per_seq, n_tiles):
    s = pl.program_id(0)
    D = h_ref.shape[2]
    tile_in_seq = jnp.minimum(s, n_tiles - 1) % tiles_per_seq
    o_refs, stat_refs = (o1_ref, o4_ref, o16_ref), (l1_ref, l4_ref, l16_ref)
    stage = iter(stages)
    take = lambda d: [next(stage) for _ in range(_stage_buffers(d))] + [None] * (2 - _stage_buffers(d))
    o_stages = [take(d) for d in DILATIONS]
    stat_stages = [take(d) for d in DILATIONS]

    def next_products():
        def product(val, row):
            return jnp.dot(val.astype(BF16), wout_ref[row:row + GROUP_WIDTH, :], preferred_element_type=F32)
        pool = product(_pool_mixer(tile_in_seq, u_ref, uh_ref, wpool_ref, pscale_ref), 0)
        scales = _group_scales(stat_refs, stat_stages)
        yield pool
        for g, dil in enumerate(DILATIONS):
            val = _natural_order(o_refs[g], dil, o_stages[g]) * scales[g]
            yield product(val, POOL_WIDTH + g * GROUP_WIDTH)

    @pl.when(s == 0)
    def _():
        carry[...] = sum(next_products())

    @pl.when(s > 0)
    def _():
        h = h_ref[0] + carry[...]
        ple = jnp.dot(p_ref[0, 0].astype(BF16), wple_ref[...], preferred_element_type=F32)
        hn = _rms(h, g2_ref[...]).astype(BF16)
        acc, nxt, products = h, 0.0, next_products()
        pieces_per_chunk = (len(DILATIONS) + 1) * FF_CHUNK // D_FF
        for c in range(0, D_FF, FF_CHUNK):
            up = jnp.dot(hn, wup_ref[:, c:c + FF_CHUNK], preferred_element_type=F32)
            act = jnp.square(jnp.maximum(up, 0.0)).astype(BF16)
            acc = acc + jnp.dot(act, wdown_ref[c:c + FF_CHUNK, :], preferred_element_type=F32)
            for _ in range(pieces_per_chunk):
                nxt = nxt + next(products)
        h = acc
        carry[...] = nxt

        hn = _rms(h, g3_ref[...]).astype(BF16)
        half_ple = 0.5 * ple
        base = h + half_ple
        pieces = []
        for c in range(0, D, GATE_CHUNK):
            half_z = jnp.dot(hn, wgate_ref[:, c:c + GATE_CHUNK], preferred_element_type=F32)
            pieces.append(base[:, c:c + GATE_CHUNK] + half_ple[:, c:c + GATE_CHUNK] * jnp.tanh(half_z))
        if final:
            ssq = sum(jnp.sum(x * x, axis=-1, keepdims=True) for x in pieces)
            inv = lax.rsqrt(ssq / D + EPS)
            pieces = [x * inv * gf_ref[:, c:c + GATE_CHUNK] for x, c in zip(pieces, range(0, D, GATE_CHUNK))]
        for x, c in zip(pieces, range(0, D, GATE_CHUNK)):
            out_ref[0, :, c:c + GATE_CHUNK] = x


def _mix(h, u, o, stats, p, layer, wpool, pscale, wout, g2, wup, wdown, g3, wgate, wple, gf, final):
    B, S, D = h.shape
    tm = ROW_TILE
    nt = S // tm
    n_tiles = B * nt
    ahead = lambda s: jnp.minimum(s, n_tiles - 1)
    behind = lambda s: jnp.maximum(s - 1, 0)
    row_b = lambda width: pl.BlockSpec((1, tm, width), lambda s: (behind(s) // nt, behind(s) % nt, 0))
    row_a = lambda width: pl.BlockSpec((1, tm, width), lambda s: (ahead(s) // nt, ahead(s) % nt, 0))
    halo = pl.BlockSpec(
        (1, POOL_HALO, POOL_WIDTH),
        lambda s: (ahead(s) // nt, jnp.maximum(ahead(s) % nt * (tm // POOL_HALO) - 1, 0), 0))
    strided = [pl.BlockSpec((1, tm // dil, dil * GROUP_WIDTH), lambda s: (ahead(s) // nt, ahead(s) % nt, 0))
               for dil in DILATIONS]
    p_spec = pl.BlockSpec((1, 1, tm, p.shape[-1]), lambda s: (layer, behind(s) // nt, behind(s) % nt, 0))
    weights = [wpool, pscale, wout, g2, wup, wdown, g3, wgate, wple, gf]
    return pl.pallas_call(
        functools.partial(_mix_kernel, final=final, tiles_per_seq=nt, n_tiles=n_tiles),
        grid=(n_tiles + 1,),
        in_specs=[row_b(D), row_a(POOL_WIDTH), halo] + strided * 2 + [p_spec]
                 + [_const_spec(w.shape[1:], layer) if w.ndim == 3 else _const_spec(w.shape) for w in weights],
        out_specs=row_b(D),
        out_shape=jax.ShapeDtypeStruct((B, S, D), F32),
        scratch_shapes=[pltpu.VMEM((tm, D), F32)]
                       + [pltpu.VMEM((GROUP_WIDTH // LANES, tm, LANES), F32)
                          for dil in DILATIONS * 2 for _ in range(_stage_buffers(dil))],
        compiler_params=pltpu.CompilerParams(
            dimension_semantics=("arbitrary",), vmem_limit_bytes=VMEM_LIMIT),
        name="mix_final" if final else "mix",
    )(h, u, u, *o, *stats, p, *weights)


def _rotary_parts(positions):
    inv_freq = ROPE_THETA ** (-jnp.arange(0, ROT_DIM, 2, dtype=F32) / ROT_DIM)
    ang = positions.astype(F32)[..., None] * inv_freq
    cs = jnp.concatenate([jnp.cos(ang), jnp.sin(ang)], axis=-1)
    hi = cs.astype(BF16)
    rest = cs - hi.astype(F32)
    mid = rest.astype(BF16)
    lo = (rest - mid.astype(F32)).astype(BF16)
    lane = np.arange(LANES)
    rotary = (lane % HEAD_DIM) < ROT_DIM
    pick = ((lane[None, :] % _HALF == np.arange(_HALF)[:, None]) & rotary[None, :]).astype(np.float32)
    sign = np.where(lane < HEAD_DIM, -1.0, 1.0).astype(np.float32)
    zero = np.zeros_like(pick)
    one_part = np.block([[pick, zero], [zero, pick * sign]])
    sel = np.concatenate([one_part] * 3, axis=0)
    return jnp.concatenate([hi, mid, lo], axis=-1), jnp.asarray(sel, BF16)


def _block_diag(w):
    g, c, _ = w.shape
    eye = jnp.eye(g, dtype=w.dtype)
    return (w[:, :, None, :] * eye[:, None, :, None]).reshape(g * c, g * c)


def kernel(x, p, positions, norm1, w_in, pool_w, pool_scale, w_out, norm2, w_up, w_down, norm3, w_gate,
           w_ple, final_norm):
    depth = w_in.shape[0]
    n_grp = len(DILATIONS)
    cs, sel = _rotary_parts(positions)
    bias = _band_bias()
    vec = lambda a: a.reshape(1, -1)
    w_in_b, w_out_b, w_up_b, w_down_b, w_gate_b, w_ple_b = _prepare_weights(
        w_in, (w_out, w_up, w_down, w_gate, w_ple), (1.0, 1.0, 1.0, 0.5, 1.0))
    h = x
    for i in range(depth):
        u, *qkv = _inproj(h, vec(norm1[i]), w_in_b, i, cs, sel)
        o, stats = [], []
        for g, dil in enumerate(DILATIONS):
            og, sg = _attention(qkv[g], qkv[n_grp + g], qkv[2 * n_grp + g], bias, dil)
            o.append(og)
            stats.append(sg)
        h = _mix(h, u, o, stats, p, i, _block_diag(pool_w[i]).astype(BF16), vec(pool_scale[i]),
                 w_out_b, vec(norm2[i]), w_up_b, w_down_b, vec(norm3[i]), w_gate_b, w_ple_b, vec(final_norm),
                 final=(i == depth - 1))
    return h
```

```python
import functools

import numpy as np
import jax
import jax.numpy as jnp
from jax import lax
from jax.experimental import pallas as pl
from jax.experimental.pallas import tpu as pltpu

F32 = jnp.float32
BF16 = jnp.bfloat16

D_MODEL = 1024
HEAD_DIM = 64
POOL_WIDTH = 256
POOL_WINDOWS = (2, 4, 8, 16)
POOL_GC = POOL_WIDTH // len(POOL_WINDOWS)
ATTN_WIDTH = D_MODEL - POOL_WIDTH
DILATIONS = (1, 4, 16)
GROUP_WIDTH = ATTN_WIDTH // len(DILATIONS)
ROT_DIM = HEAD_DIM // 4
ROPE_THETA = 500000.0
BLK = 128
D_FF = 4 * D_MODEL
EPS = 1e-6
MASK_VALUE = -1e30

LANES = 128
POOL_HALO = 16
ROW_TILE = 512
INPROJ_TILE = 1024
ATTN_ROWS = 4096
ATTN_MIN_ROWS = 2048
STAT_SPLIT = HEAD_DIM // 2
FF_CHUNK = 1024
GATE_CHUNK = 256
VMEM_LIMIT = 56 * 1024 * 1024
WEIGHT_SLABS = 8
SUBLANE_STRIDE = 4


def _rms(x, g):
    return x * lax.rsqrt(jnp.mean(x * x, axis=-1, keepdims=True) + EPS) * g


def _const_spec(shape, layer=None):
    zeros = (0,) * len(shape)
    if layer is None:
        return pl.BlockSpec(shape, lambda *_: zeros, pipeline_mode=pl.Buffered(1))
    return pl.BlockSpec((None,) + tuple(shape), lambda *_: (layer,) + zeros, pipeline_mode=pl.Buffered(1))


_HALF = ROT_DIM // 2
_QKV_ORDER = ((2, 2), (2, 1), (0, 0), (0, 2), (0, 1), (1, 2), (1, 0), (1, 1), (2, 0))
BLOCKS_AHEAD = 2


def _residue_rows(src, mid, dil):
    tm = src.shape[0]
    if dil <= SUBLANE_STRIDE:
        for r in range(dil):
            yield r, src[pl.ds(r, tm // dil, stride=dil), :]
        return
    first, second = SUBLANE_STRIDE, dil // SUBLANE_STRIDE
    part = tm // first
    for rl in range(first):
        mid[rl * part:(rl + 1) * part, :] = src[pl.ds(rl, part, stride=first), :]
    for r in range(dil):
        rh, rl = divmod(r, first)
        yield r, mid[pl.ds(rl * part + rh, tm // dil, stride=second), :]


def _prep_kernel(*refs, scales):
    n = len(refs) // 2
    (win_ref, *plain_in), (win_out, *plain_out) = refs[:n], refs[n:]
    for w_ref, o_ref, scale in zip(plain_in, plain_out, scales):
        w = w_ref[0]
        o_ref[0] = (w if scale == 1.0 else w * scale).astype(o_ref.dtype)
    lane = lax.broadcasted_iota(jnp.int32, (win_ref.shape[1], LANES), 1)
    swap = HEAD_DIM - _HALF
    from_high = (lane >= _HALF) & (lane < ROT_DIM)
    from_low = (lane >= HEAD_DIM) & (lane < HEAD_DIM + _HALF)
    for c in range(0, win_ref.shape[2], LANES):
        w = win_ref[0, :, c:c + LANES]
        if POOL_WIDTH <= c < POOL_WIDTH + 2 * ATTN_WIDTH:
            w = jnp.where(from_high, pltpu.roll(w, LANES - swap, 1), jnp.where(from_low, pltpu.roll(w, swap, 1), w))
        if POOL_WIDTH <= c < POOL_WIDTH + ATTN_WIDTH:
            w = w * HEAD_DIM ** -0.5
        win_out[0, :, c:c + LANES] = w.astype(win_out.dtype)


def _prepare_weights(w_in, plain, scales):
    weights = (w_in,) + tuple(plain)
    depth = w_in.shape[0]
    slab = lambda w: pl.BlockSpec((1, w.shape[1] // WEIGHT_SLABS, w.shape[2]), lambda l, i: (l, i, 0))
    return pl.pallas_call(
        functools.partial(_prep_kernel, scales=tuple(scales)),
        grid=(depth, WEIGHT_SLABS),
        in_specs=[slab(w) for w in weights],
        out_specs=[slab(w) for w in weights],
        out_shape=[jax.ShapeDtypeStruct(w.shape, BF16) for w in weights],
        compiler_params=pltpu.CompilerParams(
            dimension_semantics=("parallel", "parallel"), vmem_limit_bytes=VMEM_LIMIT),
        name="weight_prep",
    )(*weights)


def _head_a_lanes(lane):
    return (lane < _HALF) | ((lane >= ROT_DIM) & (lane < HEAD_DIM + _HALF))


def _inproj_kernel(x_ref, g_ref, w_ref, cs_ref, sel_ref, u_ref, *refs):
    n_grp = len(DILATIONS)
    qkv_refs, (hnbuf, *rest) = refs[:3 * n_grp], refs[3 * n_grp:]
    zbufs, stages = rest[:BLOCKS_AHEAD], rest[BLOCKS_AHEAD:]
    tm = x_ref.shape[1]
    s = pl.program_id(0)

    def column(kind, g):
        return POOL_WIDTH + kind * ATTN_WIDTH + g * GROUP_WIDTH

    def prepare_next():
        hn_next = _rms(x_ref[0], g_ref[...]).astype(BF16)
        for ahead_ref, block in zip(zbufs, _QKV_ORDER):
            col = column(*block)
            ahead_ref[...] = jnp.dot(hn_next, w_ref[:, col:col + GROUP_WIDTH], preferred_element_type=F32)
        hnbuf[...] = hn_next

    @pl.when(s == 0)
    def _():
        prepare_next()

    @pl.when(s > 0)
    def _():
        table = jnp.dot(cs_ref[0], sel_ref[...], preferred_element_type=F32)
        lane = lax.broadcasted_iota(jnp.int32, (1, LANES), 1)
        cos_t = table[:, :LANES] + ((lane % HEAD_DIM) >= ROT_DIM).astype(F32)
        sin_t = table[:, LANES:]
        hn = hnbuf[...]

        for b, (kind, g) in enumerate(_QKV_ORDER):
            out_ref = qkv_refs[kind * n_grp + g]
            if b < len(zbufs):
                z = zbufs[b][...]
            else:
                col = column(kind, g)
                z = jnp.dot(hn, w_ref[:, col:col + GROUP_WIDTH], preferred_element_type=F32)
            halves = [z[:, c:c + LANES] for c in range(0, GROUP_WIDTH, LANES)]
            if kind < 2:
                halves = [z_h * cos_t + pltpu.roll(z_h, LANES // 2, 1) * sin_t for z_h in halves]
            dil = DILATIONS[g]
            if dil == 1:
                out_ref[0] = jnp.concatenate(halves, axis=-1).astype(out_ref.dtype)
                continue
            src, mid = stages[2 * (b % 2)], stages[2 * (b % 2) + 1]
            for t, z_h in enumerate(halves):
                src[t] = z_h
                for r, rows in _residue_rows(src.at[t], mid.at[t], dil):
                    c = r * GROUP_WIDTH + t * LANES
                    out_ref[0, :, c:c + LANES] = rows.astype(out_ref.dtype)
        u_ref[0] = jnp.dot(hn, w_ref[:, :POOL_WIDTH], preferred_element_type=F32)
        prepare_next()


def _inproj(x, g, w_in, layer, cs, sel):
    B, S, D = x.shape
    n_in = w_in.shape[2]
    tm = INPROJ_TILE
    nt = S // tm
    n_tiles = B * nt
    ahead = lambda s: jnp.minimum(s, n_tiles - 1)
    behind = lambda s: jnp.maximum(s - 1, 0)
    row = lambda width, tile: pl.BlockSpec((1, tm, width), lambda s: (tile(s) // nt, tile(s) % nt, 0))
    strided = lambda dil: pl.BlockSpec((1, tm // dil, dil * GROUP_WIDTH),
                                       lambda s: (behind(s) // nt, behind(s) % nt, 0))
    out_shape = [jax.ShapeDtypeStruct((B, S, POOL_WIDTH), F32)]
    out_shape += [jax.ShapeDtypeStruct((B, S // dil, dil * GROUP_WIDTH), BF16) for dil in DILATIONS] * 3
    return pl.pallas_call(
        _inproj_kernel,
        grid=(n_tiles + 1,),
        in_specs=[row(D, ahead), _const_spec((1, D)), _const_spec((D, n_in), layer), row(cs.shape[-1], behind),
                  _const_spec(sel.shape)],
        out_specs=[row(POOL_WIDTH, behind)] + [strided(dil) for dil in DILATIONS] * 3,
        out_shape=out_shape,
        scratch_shapes=[pltpu.VMEM((tm, D), BF16)] + [pltpu.VMEM((tm, GROUP_WIDTH), F32)] * BLOCKS_AHEAD
                       + [pltpu.VMEM((GROUP_WIDTH // LANES, tm, LANES), F32)] * 4,
        compiler_params=pltpu.CompilerParams(
            dimension_semantics=("arbitrary",), vmem_limit_bytes=VMEM_LIMIT),
        name="inproj",
    )(x, g, w_in, cs, sel)


def _attn_kernel(q_ref, k_ref, kh_ref, v_ref, vh_ref, bias_ref, o_ref, stat_ref, kfirst, vfirst):
    rows, cols = q_ref.shape[1], q_ref.shape[2]
    first_chunk = (pl.program_id(1) == 0).astype(jnp.int32)

    kfirst[:BLK] = kh_ref[0]
    kfirst[BLK:] = k_ref[0, :BLK]
    vfirst[:BLK] = vh_ref[0]
    vfirst[BLK:] = v_ref[0, :BLK]

    def window(ref, first, i, c):
        if i == 0:
            return first[:, c:c + LANES]
        return ref[0, (i - 1) * BLK:(i + 1) * BLK, c:c + LANES]

    lane = lax.broadcasted_iota(jnp.int32, (BLK, LANES), 1)
    head0 = lane < HEAD_DIM
    head0_qk = _head_a_lanes(lane)
    stat_is_max = (lane % HEAD_DIM) < STAT_SPLIT
    ones = jnp.ones((2 * BLK, LANES), BF16)

    def unit(i, c):
        r0 = i * BLK
        q = q_ref[0, r0:r0 + BLK, c:c + LANES]
        kwin = window(k_ref, kfirst, i, c)
        vwin = window(v_ref, vfirst, i, c)
        zero = jnp.zeros_like(q)
        q2 = jnp.concatenate([jnp.where(head0_qk, q, zero), jnp.where(head0_qk, zero, q)], axis=0)
        s = lax.dot_general(q2, kwin, (((1,), (1,)), ((), ())), preferred_element_type=F32)
        s = s + bias_ref[first_chunk if i == 0 else 0]
        m = jnp.max(s, axis=-1, keepdims=True)
        e = jnp.exp(s - m).astype(BF16)
        pv = jnp.dot(e, jnp.concatenate([vwin, ones], axis=1), preferred_element_type=F32)
        o_ref[0, r0:r0 + BLK, c:c + LANES] = jnp.where(
            head0, pv[:BLK, :LANES], pv[BLK:, :LANES]).astype(o_ref.dtype)
        m_both = jnp.where(head0, jnp.broadcast_to(m[:BLK], (BLK, LANES)), jnp.broadcast_to(m[BLK:], (BLK, LANES)))
        l_both = jnp.where(head0, pv[:BLK, LANES:], pv[BLK:, LANES:])
        stat_ref[0, r0:r0 + BLK, c:c + LANES] = jnp.where(stat_is_max, m_both, l_both)

    for i in range(rows // BLK):
        for c in range(0, cols, LANES):
            unit(i, c)


def _attention(q, k, v, bias, dil):
    B, L, width = q.shape
    W = width // dil
    rows = min(L, max(ATTN_ROWS // dil, ATTN_MIN_ROWS))
    cols = (ATTN_ROWS // rows) * W
    hist_blocks = rows // BLK
    main = pl.BlockSpec((1, rows, cols), lambda b, n, r: (b, n, r))
    hist = pl.BlockSpec((1, BLK, cols), lambda b, n, r: (b, jnp.maximum(n * hist_blocks - 1, 0), r))
    return pl.pallas_call(
        _attn_kernel,
        grid=(B, L // rows, dil * W // cols),
        in_specs=[main, main, hist, main, hist, _const_spec(bias.shape)],
        out_specs=[main, main],
        out_shape=[jax.ShapeDtypeStruct((B, L, dil * W), BF16), jax.ShapeDtypeStruct((B, L, dil * W), F32)],
        scratch_shapes=[pltpu.VMEM((2 * BLK, cols), BF16), pltpu.VMEM((2 * BLK, cols), BF16)],
        compiler_params=pltpu.CompilerParams(
            dimension_semantics=("parallel", "arbitrary", "parallel"), vmem_limit_bytes=VMEM_LIMIT),
        name=f"attn_d{dil}",
    )(q, k, k, v, v, bias)


def _band_bias():
    a = np.arange(2 * BLK)[:, None] % BLK
    c = np.arange(2 * BLK)[None, :]
    band = (c >= a) & (c <= a + BLK)
    both = np.stack([band, band & (c >= BLK)])
    return jnp.asarray(np.where(both, 0.0, MASK_VALUE), F32)


def _stage_buffers(dil):
    return 0 if dil == 1 else 1 if dil <= SUBLANE_STRIDE else 2


def _natural_order(ref, dil, stage):
    if dil == 1:
        return ref[0].astype(F32)
    n = ref.shape[1]
    tm = n * dil
    tiles = GROUP_WIDTH // LANES
    dst, mid = stage
    first = min(dil, SUBLANE_STRIDE)
    second, part = dil // first, tm // first
    for t in range(tiles):
        for r in range(dil):
            rh, rl = divmod(r, first)
            c = r * GROUP_WIDTH + t * LANES
            rows = ref[0, :, c:c + LANES].astype(F32)
            if second == 1:
                dst[t, pl.ds(r, n, stride=dil), :] = rows
            else:
                mid[t, pl.ds(rl * part + rh, n, stride=second), :] = rows
        if second > 1:
            for rl in range(first):
                dst[t, pl.ds(rl, part, stride=first), :] = mid[t, rl * part:(rl + 1) * part, :]
    return jnp.concatenate([dst[t] for t in range(tiles)], axis=-1)


def _pool_mixer(tile_in_seq, u_ref, uh_ref, wpool_ref, pscale_ref):
    tm = u_ref.shape[1]
    hist = jnp.where(tile_in_seq == 0, 0.0, uh_ref[0])
    ub = jnp.concatenate([hist, u_ref[0]], axis=0)
    s2 = ub + pltpu.roll(ub, 1, 0)
    s4 = s2 + pltpu.roll(s2, 2, 0)
    s8 = s4 + pltpu.roll(s4, 4, 0)
    s16 = s8 + pltpu.roll(s8, 8, 0)
    t_head = jnp.where(tile_in_seq == 0, lax.broadcasted_iota(jnp.int32, (POOL_HALO, LANES), 0) + 1, POOL_HALO + 1)
    lane = lax.broadcasted_iota(jnp.int32, (tm, LANES), 1)
    low = lane < POOL_GC
    tile = lambda s, c: s[POOL_HALO:, c:c + LANES]

    def mean(s, c, w):
        x = tile(s, c)
        inv_head = 1.0 / jnp.minimum(t_head, w).astype(F32)
        return jnp.concatenate([x[:POOL_HALO] * inv_head, x[POOL_HALO:] * (1.0 / w)], axis=0)

    w2, w4, w8, w16 = POOL_WINDOWS
    y_lo = jnp.where(low, mean(s2, 0, w2), mean(s4, 0, w4)) - tile(ub, 0)
    y_hi = jnp.where(low, mean(s8, LANES, w8), mean(s16, LANES, w16)) - tile(ub, LANES)
    y = jnp.concatenate([y_lo, y_hi], axis=-1).astype(BF16)
    return jnp.dot(y, wpool_ref[...], preferred_element_type=F32) * pscale_ref[...]


def _group_scales(stat_refs, stages):
    stat = [_natural_order(r, d, stage) for r, d, stage in zip(stat_refs, DILATIONS, stages)]
    lane = lax.broadcasted_iota(jnp.int32, stat[0].shape, 1)
    is_max = (lane % HEAD_DIM) < STAT_SPLIT
    mx = jnp.maximum(jnp.maximum(stat[0], stat[1]), stat[2])
    e_on_max = [jnp.exp(s - mx) for s in stat]
    e_on_sum = [pltpu.roll(e, STAT_SPLIT, 1) for e in e_on_max]
    inv_on_sum = 1.0 / (stat[0] * e_on_sum[0] + stat[1] * e_on_sum[1] + stat[2] * e_on_sum[2])
    inv_on_max = pltpu.roll(inv_on_sum, GROUP_WIDTH - STAT_SPLIT, 1)
    return [jnp.where(is_max, a * inv_on_max, b * inv_on_sum) for a, b in zip(e_on_max, e_on_sum)]


def _mix_kernel(h_ref, u_ref, uh_ref, o1_ref, o4_ref, o16_ref, l1_ref, l4_ref, l16_ref, p_ref,
                wpool_ref, pscale_ref, wout_ref, g2_ref, wup_ref, wdown_ref, g3_ref, wgate_ref, wple_ref,
                gf_ref, out_ref, carry, *stages, final, tiles_per_seq, n_tiles):
    s = pl.program_id(0)
    D = h_ref.shape[2]
    tile_in_seq = jnp.minimum(s, n_tiles - 1) % tiles_per_seq
    o_refs, stat_refs = (o1_ref, o4_ref, o16_ref), (l1_ref, l4_ref, l16_ref)
    stage = iter(stages)
    take = lambda d: [next(stage) for _ in range(_stage_buffers(d))] + [None] * (2 - _stage_buffers(d))
    o_stages = [take(d) for d in DILATIONS]
    stat_stages = [take(d) for d in DILATIONS]

    def next_products():
        def product(val, row):
            return jnp.dot(val.astype(BF16), wout_ref[row:row + GROUP_WIDTH, :], preferred_element_type=F32)
        pool = product(_pool_mixer(tile_in_seq, u_ref, uh_ref, wpool_ref, pscale_ref), 0)
        scales = _group_scales(stat_refs, stat_stages)
        yield pool
        for g, dil in enumerate(DILATIONS):
            val = _natural_order(o_refs[g], dil, o_stages[g]) * scales[g]
            yield product(val, POOL_WIDTH + g * GROUP_WIDTH)

    @pl.when(s == 0)
    def _():
        carry[...] = sum(next_products())

    @pl.when(s > 0)
    def _():
        h = h_ref[0] + carry[...]
        ple = jnp.dot(p_ref[0, 0].astype(BF16), wple_ref[...], preferred_element_type=F32)
        hn = _rms(h, g2_ref[...]).astype(BF16)
        acc, nxt, products = h, 0.0, next_products()
        pieces_per_chunk = (len(DILATIONS) + 1) * FF_CHUNK // D_FF
        for c in range(0, D_FF, FF_CHUNK):
            up = jnp.dot(hn, wup_ref[:, c:c + FF_CHUNK], preferred_element_type=F32)
            act = jnp.square(jnp.maximum(up, 0.0)).astype(BF16)
            acc = acc + jnp.dot(act, wdown_ref[c:c + FF_CHUNK, :], preferred_element_type=F32)
            for _ in range(pieces_per_chunk):
                nxt = nxt + next(products)
        h = acc
        carry[...] = nxt

        hn = _rms(h, g3_ref[...]).astype(BF16)
        half_ple = 0.5 * ple
        base = h + half_ple
        pieces = []
        for c in range(0, D, GATE_CHUNK):
            half_z = jnp.dot(hn, wgate_ref[:, c:c + GATE_CHUNK], preferred_element_type=F32)
            pieces.append(base[:, c:c + GATE_CHUNK] + half_ple[:, c:c + GATE_CHUNK] * jnp.tanh(half_z))
        if final:
            ssq = sum(jnp.sum(x * x, axis=-1, keepdims=True) for x in pieces)
            inv = lax.rsqrt(ssq / D + EPS)
            pieces = [x * inv * gf_ref[:, c:c + GATE_CHUNK] for x, c in zip(pieces, range(0, D, GATE_CHUNK))]
        for x, c in zip(pieces, range(0, D, GATE_CHUNK)):
            out_ref[0, :, c:c + GATE_CHUNK] = x


def _mix(h, u, o, stats, p, layer, wpool, pscale, wout, g2, wup, wdown, g3, wgate, wple, gf, final):
    B, S, D = h.shape
    tm = ROW_TILE
    nt = S // tm
    n_tiles = B * nt
    ahead = lambda s: jnp.minimum(s, n_tiles - 1)
    behind = lambda s: jnp.maximum(s - 1, 0)
    row_b = lambda width: pl.BlockSpec((1, tm, width), lambda s: (behind(s) // nt, behind(s) % nt, 0))
    row_a = lambda width: pl.BlockSpec((1, tm, width), lambda s: (ahead(s) // nt, ahead(s) % nt, 0))
    halo = pl.BlockSpec(
        (1, POOL_HALO, POOL_WIDTH),
        lambda s: (ahead(s) // nt, jnp.maximum(ahead(s) % nt * (tm // POOL_HALO) - 1, 0), 0))
    strided = [pl.BlockSpec((1, tm // dil, dil * GROUP_WIDTH), lambda s: (ahead(s) // nt, ahead(s) % nt, 0))
               for dil in DILATIONS]
    p_spec = pl.BlockSpec((1, 1, tm, p.shape[-1]), lambda s: (layer, behind(s) // nt, behind(s) % nt, 0))
    weights = [wpool, pscale, wout, g2, wup, wdown, g3, wgate, wple, gf]
    return pl.pallas_call(
        functools.partial(_mix_kernel, final=final, tiles_per_seq=nt, n_tiles=n_tiles),
        grid=(n_tiles + 1,),
        in_specs=[row_b(D), row_a(POOL_WIDTH), halo] + strided * 2 + [p_spec]
                 + [_const_spec(w.shape[1:], layer) if w.ndim == 3 else _const_spec(w.shape) for w in weights],
        out_specs=row_b(D),
        out_shape=jax.ShapeDtypeStruct((B, S, D), F32),
        scratch_shapes=[pltpu.VMEM((tm, D), F32)]
                       + [pltpu.VMEM((GROUP_WIDTH // LANES, tm, LANES), F32)
                          for dil in DILATIONS * 2 for _ in range(_stage_buffers(dil))],
        compiler_params=pltpu.CompilerParams(
            dimension_semantics=("arbitrary",), vmem_limit_bytes=VMEM_LIMIT),
        name="mix_final" if final else "mix",
    )(h, u, u, *o, *stats, p, *weights)


def _rotary_parts(positions):
    inv_freq = ROPE_THETA ** (-jnp.arange(0, ROT_DIM, 2, dtype=F32) / ROT_DIM)
    ang = positions.astype(F32)[..., None] * inv_freq
    cs = jnp.concatenate([jnp.cos(ang), jnp.sin(ang)], axis=-1)
    hi = cs.astype(BF16)
    rest = cs - hi.astype(F32)
    mid = rest.astype(BF16)
    lo = (rest - mid.astype(F32)).astype(BF16)
    lane = np.arange(LANES)
    rotary = (lane % HEAD_DIM) < ROT_DIM
    pick = ((lane[None, :] % _HALF == np.arange(_HALF)[:, None]) & rotary[None, :]).astype(np.float32)
    sign = np.where(lane < HEAD_DIM, -1.0, 1.0).astype(np.float32)
    zero = np.zeros_like(pick)
    one_part = np.block([[pick, zero], [zero, pick * sign]])
    sel = np.concatenate([one_part] * 3, axis=0)
    return jnp.concatenate([hi, mid, lo], axis=-1), jnp.asarray(sel, BF16)


def _block_diag(w):
    g, c, _ = w.shape
    eye = jnp.eye(g, dtype=w.dtype)
    return (w[:, :, None, :] * eye[:, None, :, None]).reshape(g * c, g * c)


def kernel(x, p, positions, norm1, w_in, pool_w, pool_scale, w_out, norm2, w_up, w_down, norm3, w_gate,
           w_ple, final_norm):
    depth = w_in.shape[0]
    n_grp = len(DILATIONS)
    cs, sel = _rotary_parts(positions)
    bias = _band_bias()
    vec = lambda a: a.reshape(1, -1)
    w_in_b, w_out_b, w_up_b, w_down_b, w_gate_b, w_ple_b = _prepare_weights(
        w_in, (w_out, w_up, w_down, w_gate, w_ple), (1.0, 1.0, 1.0, 0.5, 1.0))
    h = x
    for i in range(depth):
        u, *qkv = _inproj(h, vec(norm1[i]), w_in_b, i, cs, sel)
        o, stats = [], []
        for g, dil in enumerate(DILATIONS):
            og, sg = _attention(qkv[g], qkv[n_grp + g], qkv[2 * n_grp + g], bias, dil)
            o.append(og)
            stats.append(sg)
        h = _mix(h, u, o, stats, p, i, _block_diag(pool_w[i]).astype(BF16), vec(pool_scale[i]),
                 w_out_b, vec(norm2[i]), w_up_b, w_down_b, vec(norm3[i]), w_gate_b, w_ple_b, vec(final_norm),
                 final=(i == depth - 1))
    return h
```
